```python
import math
import jax
import jax.numpy as jnp
from jax import lax
import numpy as np


D_MODEL = 1024
BATCH = 4
SEQ = 4096
DEPTH = 4

GRID_W = 64
CTX_LEN = 256
N_MOD = 9
EPS = 1e-6
D_FF = 2816
MIX_W = D_MODEL
ATTN_W = MIX_W // 2
POOL_W = MIX_W // 4
S5_W = MIX_W // 4
HEAD_DIM = 64
N_Q_HEADS = ATTN_W // HEAD_DIM
N_KV_HEADS = N_Q_HEADS // 4
Q_PER_KV = N_Q_HEADS // N_KV_HEADS
KV_W = N_KV_HEADS * HEAD_DIM
Q_BLOCK = 128
ROPE_THETA = 10000.0
AXIS_DIM = HEAD_DIM // 2
AXIS_FREQS = AXIS_DIM // 2
POOL_WINDOWS = (2, 4, 8, 16)
POOL_CH = POOL_W // len(POOL_WINDOWS)
S5_CH = 16
S5_GROUPS = S5_W // S5_CH
S5_STATE = 64
S5_DT_MIN = 0.001
S5_DT_MAX = 0.1
IN_W = ATTN_W + 2 * KV_W + POOL_W + S5_W
SPLITS = (ATTN_W, ATTN_W + KV_W, ATTN_W + 2 * KV_W, ATTN_W + 2 * KV_W + POOL_W)

kernel_name = 'hybrid_flow_backbone_pool_s5_gqa'


def rmsnorm(x, g):
    xf = x.astype(jnp.float32)
    y = xf * lax.rsqrt(jnp.mean(xf * xf, axis=-1, keepdims=True) + EPS)
    return (y * g.astype(jnp.float32)).astype(x.dtype)


def adaln(cond, w, b):
    m = jax.nn.silu(cond) @ w + b
    return jnp.split(m[..., None, :], N_MOD, axis=-1)


def modulate(s, g, shift, scale):
    return rmsnorm(s, g) * (1 + scale) + shift


def swiglu(h, w_gate, w_up, w_down):
    return (jax.nn.silu(h @ w_gate) * (h @ w_up)) @ w_down


def axial_rope_tables(rows):
    row = jnp.broadcast_to(jnp.arange(rows, dtype=jnp.float32)[:, None], (rows, GRID_W)).reshape(-1)
    col = jnp.broadcast_to(jnp.arange(GRID_W, dtype=jnp.float32)[None, :], (rows, GRID_W)).reshape(-1)
    inv = ROPE_THETA ** (-2.0 * jnp.arange(AXIS_FREQS, dtype=jnp.float32) / AXIS_DIM)
    ang = jnp.concatenate([row[:, None] * inv, col[:, None] * inv], axis=-1)
    return jnp.cos(ang), jnp.sin(ang)


def apply_axial_rope(x, cos, sin):
    shape = x.shape
    xr = x.reshape(*shape[:-1], 2, 2, AXIS_FREQS)
    x1, x2 = xr[..., 0, :], xr[..., 1, :]
    bshape = (cos.shape[0],) + (1,) * (x.ndim - 3) + (2, AXIS_FREQS)
    cb = cos.reshape(bshape).astype(x.dtype)
    sb = sin.reshape(bshape).astype(x.dtype)
    out = jnp.stack([x1 * cb - x2 * sb, x2 * cb + x1 * sb], axis=-2)
    return out.reshape(shape)


def gqa_attend(q, k, v):
    s = jnp.einsum('bqgrd,bkgd->bgrqk', q, k).astype(jnp.float32) * (HEAD_DIM ** -0.5)
    p = jax.nn.softmax(s, axis=-1).astype(v.dtype)
    return jnp.einsum('bgrqk,bkgd->bqgrd', p, v)


def attention_branch(pc, pl, q_norm, k_norm, cos, sin, need_ctx_out):
    (qc, kc, vc), (ql, kl, vl) = pc, pl
    B, L, _ = ql.shape
    Lc = kc.shape[1]
    kc = rmsnorm(kc.reshape(B, Lc, N_KV_HEADS, HEAD_DIM), k_norm)
    vc = vc.reshape(B, Lc, N_KV_HEADS, HEAD_DIM)
    ql = apply_axial_rope(rmsnorm(ql.reshape(B, L, N_KV_HEADS, Q_PER_KV, HEAD_DIM), q_norm), cos, sin)
    kl = apply_axial_rope(rmsnorm(kl.reshape(B, L, N_KV_HEADS, HEAD_DIM), k_norm), cos, sin)
    vl = vl.reshape(B, L, N_KV_HEADS, HEAD_DIM)
    k_all = jnp.concatenate([kc, kl], axis=1)
    v_all = jnp.concatenate([vc, vl], axis=1)
    q_blocks = jnp.moveaxis(ql.reshape(B, L // Q_BLOCK, Q_BLOCK, N_KV_HEADS, Q_PER_KV, HEAD_DIM), 1, 0)
    o_blocks = lax.map(lambda qb: gqa_attend(qb, k_all, v_all), q_blocks)
    out_l = jnp.moveaxis(o_blocks, 0, 1).reshape(B, L, ATTN_W)
    out_c = None
    if need_ctx_out:
        qc = rmsnorm(qc.reshape(B, Lc, N_KV_HEADS, Q_PER_KV, HEAD_DIM), q_norm)
        out_c = gqa_attend(qc, kc, vc).reshape(B, Lc, ATTN_W)
    return out_c, out_l


def centred_window_mean(u, w):
    L = u.shape[1]
    uf = u.astype(jnp.float32)
    cs = jnp.concatenate([jnp.zeros_like(uf[:, :1]), jnp.cumsum(uf, axis=1)], axis=1)
    t = jnp.arange(L)
    lo = jnp.clip(t - w // 2, 0, L)
    hi = jnp.clip(t + (w - w // 2), 0, L)
    cnt = (hi - lo).astype(jnp.float32)[None, :, None]
    return ((cs[:, hi] - cs[:, lo]) / cnt).astype(u.dtype)


def pool_branch(u, pool_w, pool_scale):
    B, L, _ = u.shape
    g = u.reshape(B, L, len(POOL_WINDOWS), POOL_CH)
    pooled = jnp.stack([centred_window_mean(g[:, :, i], w) for i, w in enumerate(POOL_WINDOWS)], axis=2) - g
    y = jnp.einsum('blgc,gcd->blgd', pooled, pool_w).reshape(B, L, POOL_W)
    return y * pool_scale


def zoh_discretise(lam_re, lam_im, log_dt, b_re, b_im):
    f32 = jnp.float32
    dt = jnp.exp(log_dt.astype(f32))[:, None]
    lr, li = lam_re.astype(f32), lam_im.astype(f32)
    mag = jnp.exp(lr * dt)
    ab_re = mag * jnp.cos(li * dt)
    ab_im = mag * jnp.sin(li * dt)
    den = lr * lr + li * li
    f_re = ((ab_re - 1) * lr + ab_im * li) / den
    f_im = (ab_im * lr - (ab_re - 1) * li) / den
    br, bi = b_re.astype(f32), b_im.astype(f32)
    bb_re = f_re[..., None] * br - f_im[..., None] * bi
    bb_im = f_re[..., None] * bi + f_im[..., None] * br
    dtype = b_re.dtype
    return ab_re.astype(dtype), ab_im.astype(dtype), bb_re.astype(dtype), bb_im.astype(dtype)


def ssm_combine(e1, e2):
    a1r, a1i, b1r, b1i = e1
    a2r, a2i, b2r, b2i = e2
    return (a2r * a1r - a2i * a1i,
            a2r * a1i + a2i * a1r,
            a2r * b1r - a2i * b1i + b2r,
            a2r * b1i + a2i * b1r + b2i)


def diag_scan(ab_re, ab_im, bb_re, bb_im, u, s0_re, s0_im, reverse):
    bu_re = jnp.einsum('gph,blgh->blgp', bb_re, u)
    bu_im = jnp.einsum('gph,blgh->blgp', bb_im, u)
    first = -1 if reverse else 0
    last = 0 if reverse else -1
    bu_re = bu_re.at[:, first].add(ab_re * s0_re - ab_im * s0_im)
    bu_im = bu_im.at[:, first].add(ab_re * s0_im + ab_im * s0_re)
    a_re = jnp.broadcast_to(ab_re, bu_re.shape)
    a_im = jnp.broadcast_to(ab_im, bu_im.shape)
    _, _, s_re, s_im = lax.associative_scan(ssm_combine, (a_re, a_im, bu_re, bu_im), axis=1, reverse=reverse)
    return s_re, s_im, s_re[:, last], s_im[:, last]


def ssm_readout(c_re, c_im, s_re, s_im):
    return jnp.einsum('ghp,blgp->blgh', c_re, s_re) - jnp.einsum('ghp,blgp->blgh', c_im, s_im)


def s5_glu(y, glu_w, glu_b):
    g = jax.nn.gelu(y)
    return g * jax.nn.sigmoid(jnp.einsum('blgh,ghk->blgk', g, glu_w) + glu_b)


def s5_branch(uc, ul, lam_re, lam_im, log_dt, b_re, b_im, c_re, c_im, d_skip, glu_w, glu_b, need_ctx_out):
    B = ul.shape[0]
    uc = uc.reshape(B, uc.shape[1], S5_GROUPS, S5_CH)
    ul = ul.reshape(B, ul.shape[1], S5_GROUPS, S5_CH)
    zero = jnp.zeros((B, S5_GROUPS, S5_STATE), ul.dtype)
    y_l = d_skip * ul
    y_c = d_skip * uc if need_ctx_out else None
    for d, reverse in enumerate((False, True)):
        ab_re, ab_im, bb_re, bb_im = zoh_discretise(lam_re[d], lam_im[d], log_dt[d], b_re[d], b_im[d])
        sc_re, sc_im, fc_re, fc_im = diag_scan(ab_re, ab_im, bb_re, bb_im, uc, zero, zero, reverse)
        sl_re, sl_im, _, _ = diag_scan(ab_re, ab_im, bb_re, bb_im, ul, fc_re, fc_im, reverse)
        y_l = y_l + ssm_readout(c_re[d], c_im[d], sl_re, sl_im)
        if need_ctx_out:
            y_c = y_c + ssm_readout(c_re[d], c_im[d], sc_re, sc_im)
    out_l = s5_glu(y_l, glu_w, glu_b).reshape(B, -1, S5_W)
    out_c = s5_glu(y_c, glu_w, glu_b).reshape(B, -1, S5_W) if need_ctx_out else None
    return out_c, out_l


def mixer(hc, hl, w_in, w_out, q_norm, k_norm, attn_out_norm, ssm_out_norm, pool_w, pool_scale, s5_params, cos, sin, need_ctx_out):
    pc = jnp.split(hc @ w_in, SPLITS, axis=-1)
    pl = jnp.split(hl @ w_in, SPLITS, axis=-1)
    attn_c, attn_l = attention_branch(pc[:3], pl[:3], q_norm, k_norm, cos, sin, need_ctx_out)
    ssm_c, ssm_l = s5_branch(pc[4], pl[4], *s5_params, need_ctx_out)

    def merge(a, p, s):
        return jnp.concatenate([rmsnorm(a, attn_out_norm), p, rmsnorm(s, ssm_out_norm)], axis=-1) @ w_out

    out_l = merge(attn_l, pool_branch(pl[3], pool_w, pool_scale), ssm_l)
    out_c = merge(attn_c, pool_branch(pc[3], pool_w, pool_scale), ssm_c) if need_ctx_out else None
    return out_c, out_l


def setup_inputs(seed: int = 0) -> dict:
    key = jax.random.key(seed)
    ks = jax.random.split(key, 32)
    D = D_MODEL
    G, H, P = S5_GROUPS, S5_CH, S5_STATE

    def nrm(k, shape, s):
        return jax.random.normal(k, shape, jnp.float32) * s

    lam_im_base = math.pi * jnp.arange(P, dtype=jnp.float32)
    return {
        'x': nrm(ks[0], (BATCH, SEQ, D), 1.0),
        'c': nrm(ks[1], (BATCH, D), 1.0),
        'ctx': nrm(ks[2], (BATCH, CTX_LEN, D), 1.0),
        'c_ctx': nrm(ks[3], (D,), 1.0),
        'w_ada': nrm(ks[4], (DEPTH, D, N_MOD * D), 0.5 * D ** -0.5),
        'b_ada': nrm(ks[5], (DEPTH, N_MOD * D), 0.02),
        'norm_sub': 1.0 + nrm(ks[6], (DEPTH, 3, D), 0.02),
        'w_ffn_gate': nrm(ks[7], (DEPTH, 2, D, D_FF), D ** -0.5),
        'w_ffn_up': nrm(ks[8], (DEPTH, 2, D, D_FF), D ** -0.5),
        'w_ffn_down': nrm(ks[9], (DEPTH, 2, D_FF, D), D_FF ** -0.5),
        'w_in': nrm(ks[10], (DEPTH, D, IN_W), D ** -0.5),
        'w_out': nrm(ks[11], (DEPTH, MIX_W, D), MIX_W ** -0.5),
        'q_norm': 1.0 + nrm(ks[12], (DEPTH, HEAD_DIM), 0.02),
        'k_norm': 1.0 + nrm(ks[13], (DEPTH, HEAD_DIM), 0.02),
        'attn_out_norm': 1.0 + nrm(ks[14], (DEPTH, ATTN_W), 0.02),
        'ssm_out_norm': 1.0 + nrm(ks[15], (DEPTH, S5_W), 0.02),
        'pool_w': nrm(ks[16], (DEPTH, len(POOL_WINDOWS), POOL_CH, POOL_CH), POOL_CH ** -0.5),
        'pool_scale': 1.0 + nrm(ks[17], (DEPTH, POOL_W), 0.1),
        's5_lam_re': -0.5 + nrm(ks[18], (DEPTH, 2, G, P), 0.01),
        's5_lam_im': lam_im_base + nrm(ks[19], (DEPTH, 2, G, P), 0.01),
        's5_log_dt': jax.random.uniform(ks[20], (DEPTH, 2, G), jnp.float32, math.log(S5_DT_MIN), math.log(S5_DT_MAX)),
        's5_b_re': nrm(ks[21], (DEPTH, 2, G, P, H), (2 * H) ** -0.5),
        's5_b_im': nrm(ks[22], (DEPTH, 2, G, P, H), (2 * H) ** -0.5),
        's5_c_re': nrm(ks[23], (DEPTH, 2, G, H, P), P ** -0.5),
        's5_c_im': nrm(ks[24], (DEPTH, 2, G, H, P), P ** -0.5),
        's5_d': nrm(ks[25], (DEPTH, G, H), 0.5),
        's5_glu_w': nrm(ks[26], (DEPTH, G, H, H), H ** -0.5),
        's5_glu_b': nrm(ks[27], (DEPTH, G, H), 0.02),
        'final_norm': 1.0 + nrm(ks[28], (D,), 0.02),
    }


def reference(x, c, ctx, c_ctx, w_ada, b_ada, norm_sub, w_ffn_gate, w_ffn_up, w_ffn_down, w_in, w_out, q_norm, k_norm, attn_out_norm, ssm_out_norm, pool_w, pool_scale, s5_lam_re, s5_lam_im, s5_log_dt, s5_b_re, s5_b_im, s5_c_re, s5_c_im, s5_d, s5_glu_w, s5_glu_b, final_norm):
    ROWS = x.shape[1] // GRID_W
    cos, sin = axial_rope_tables(ROWS)
    z = ctx
    for l in range(DEPTH):
        need_ctx_out = l < DEPTH - 1
        mx = adaln(c, w_ada[l], b_ada[l])
        mz = adaln(c_ctx, w_ada[l], b_ada[l])
        ffn1 = (w_ffn_gate[l, 0], w_ffn_up[l, 0], w_ffn_down[l, 0])
        ffn2 = (w_ffn_gate[l, 1], w_ffn_up[l, 1], w_ffn_down[l, 1])
        x = x + 0.5 * mx[2] * swiglu(modulate(x, norm_sub[l, 0], mx[0], mx[1]), *ffn1)
        z = z + 0.5 * mz[2] * swiglu(modulate(z, norm_sub[l, 0], mz[0], mz[1]), *ffn1)
        s5_params = (s5_lam_re[l], s5_lam_im[l], s5_log_dt[l], s5_b_re[l], s5_b_im[l], s5_c_re[l], s5_c_im[l], s5_d[l], s5_glu_w[l], s5_glu_b[l])
        out_c, out_l = mixer(modulate(z, norm_sub[l, 1], mz[3], mz[4]), modulate(x, norm_sub[l, 1], mx[3], mx[4]), w_in[l], w_out[l], q_norm[l], k_norm[l], attn_out_norm[l], ssm_out_norm[l], pool_w[l], pool_scale[l], s5_params, cos, sin, need_ctx_out)
        x = x + mx[5] * out_l
        x = x + 0.5 * mx[8] * swiglu(modulate(x, norm_sub[l, 2], mx[6], mx[7]), *ffn2)
        if need_ctx_out:
            z = z + mz[5] * out_c
            z = z + 0.5 * mz[8] * swiglu(modulate(z, norm_sub[l, 2], mz[6], mz[7]), *ffn2)
    return rmsnorm(x, final_norm)
```

```python
import functools
import math

import jax
import jax.numpy as jnp
from jax import lax
from jax.experimental import pallas as pl
from jax.experimental.pallas import tpu as pltpu

F32 = jnp.float32
BF16 = jnp.bfloat16

D_MODEL = 1024
DEPTH = 4
GRID_W = 64
N_MOD = 9
EPS = 1e-6
D_FF = 2816
ATTN_W = 512
POOL_W = 256
S5_W = 256
HEAD_DIM = 64
N_Q_HEADS = 8
N_KV_HEADS = 2
Q_PER_KV = 4
KV_W = 128
ROPE_THETA = 10000.0
AXIS_DIM = 32
AXIS_FREQS = 16
POOL_WINDOWS = (2, 4, 8, 16)
POOL_CH = 64
S5_CH = 16
S5_GROUPS = 16
S5_STATE = 64
S5_CHANNELS = S5_GROUPS * S5_STATE
IN_W = ATTN_W + 2 * KV_W + POOL_W + S5_W

LANES = 128
SUBLANES = 8
POOL_HALO = 8
VMEM_LIMIT = 56 * 1024 * 1024


def _params(sem):
    return pltpu.CompilerParams(dimension_semantics=sem, vmem_limit_bytes=VMEM_LIMIT)


def _dot(a, b):
    return jnp.dot(a, b, preferred_element_type=F32)


def _silu(x):
    return x * jax.nn.sigmoid(x)


def _rmsnorm(x, g):
    return x * lax.rsqrt(jnp.mean(x * x, axis=-1, keepdims=True) + EPS) * g


def _modulate(x, g, shift, scale):
    return _rmsnorm(x, g) * (1.0 + scale) + shift


def _adaln_kernel(cond_ref, w_ref, b_ref, o_ref):
    s = _silu(cond_ref[...]).astype(BF16)
    o_ref[...] = _dot(s, w_ref[...].astype(BF16)) + b_ref[...]


def _adaln(cond, w_ada, b_ada):
    d = D_MODEL
    tn = 1024
    nj = (N_MOD * d) // tn
    return pl.pallas_call(
        _adaln_kernel,
        grid=(DEPTH, nj),
        in_specs=[
            pl.BlockSpec((SUBLANES, d), lambda l, j: (0, 0)),
            pl.BlockSpec((None, d, tn), lambda l, j: (l, 0, j)),
            pl.BlockSpec((None, 1, tn), lambda l, j: (l, 0, j)),
        ],
        out_specs=pl.BlockSpec((None, SUBLANES, tn), lambda l, j: (l, 0, j)),
        out_shape=jax.ShapeDtypeStruct((DEPTH, SUBLANES, N_MOD * d), F32),
        compiler_params=_params(("arbitrary", "arbitrary")),
        name="adaln",
    )(cond, w_ada, b_ada.reshape(DEPTH, 1, N_MOD * d))


def _zoh_kernel(lr_ref, li_ref, ldt_ref, br_ref, bi_ref, abr_ref, abi_ref, bbr_ref, bbi_ref):
    dt = jnp.exp(ldt_ref[...])
    lr = lr_ref[...]
    li = li_ref[...]
    mag = jnp.exp(lr * dt)
    ab_re = mag * jnp.cos(li * dt)
    ab_im = mag * jnp.sin(li * dt)
    den = lr * lr + li * li
    f_re = ((ab_re - 1.0) * lr + ab_im * li) / den
    f_im = (ab_im * lr - (ab_re - 1.0) * li) / den
    br = br_ref[...]
    bi = bi_ref[...]
    abr_ref[...] = ab_re
    abi_ref[...] = ab_im
    bbr_ref[...] = f_re * br - f_im * bi
    bbi_ref[...] = f_re * bi + f_im * br


def _s5_tables(lam_re, lam_im, log_dt, b_re, b_im, c_re, c_im):
    g, h, p = S5_GROUPS, S5_CH, S5_STATE
    rows = DEPTH * 2 * g * h

    def rep(a):
        return jnp.broadcast_to(a[:, :, :, None, :], (DEPTH, 2, g, h, p)).reshape(rows, p)

    ldt = jnp.broadcast_to(log_dt[:, :, :, None, None], (DEPTH, 2, g, h, p)).reshape(rows, p)
    bt_re = jnp.swapaxes(b_re, -1, -2).reshape(rows, p)
    bt_im = jnp.swapaxes(b_im, -1, -2).reshape(rows, p)
    shp = jax.ShapeDtypeStruct((rows, p), F32)
    ab_re, ab_im, bb_re, bb_im = pl.pallas_call(
        _zoh_kernel, out_shape=(shp, shp, shp, shp), name="s5_zoh",
    )(rep(lam_re), rep(lam_im), ldt, bt_re, bt_im)

    def chan(a):
        return a.reshape(DEPTH, 2, g, h, p)[:, :, :, 0, :].reshape(DEPTH, 2, g * p)

    avec = jnp.stack([chan(ab_re), chan(ab_im)], axis=2).reshape(DEPTH, 2, 2, 1, g * p)
    eye = jnp.eye(g, dtype=F32)

    def in_blk(bt):
        bt = bt.reshape(DEPTH, 2, g, h, p)
        return jnp.einsum('gk,ldghp->ldghkp', eye, bt).reshape(DEPTH, 2, g * h, g * p)

    def out_blk(cc):
        return jnp.einsum('gk,ldkhp->ldgpkh', eye, cc).reshape(DEPTH, 2, g * p, g * h)

    bblk = jnp.concatenate([in_blk(bb_re), in_blk(bb_im)], axis=-1).astype(BF16)
    cblk = jnp.concatenate([out_blk(c_re), out_blk(-c_im)], axis=-2).astype(BF16)
    return avec, bblk, cblk


def _ffn_kernel(x_ref, mod_ref, g_ref, wg_ref, wu_ref, wd_ref, *rest, final):
    if final:
        fg_ref, o_ref = rest
    else:
        (o_ref,) = rest
    x = x_ref[...]
    h = _modulate(x, g_ref[...], mod_ref[0], mod_ref[1]).astype(BF16)
    a = _dot(h, wg_ref[...])
    u = _dot(h, wu_ref[...])
    act = (_silu(a) * u).astype(BF16)
    y = x + (0.5 * mod_ref[2]) * _dot(act, wd_ref[...])
    if final:
        y = _rmsnorm(y, fg_ref[...])
    o_ref[...] = y


def _ffn(x, mods, norm_sub, wg, wu, wd, l, sub, which, row_fn, tm, final_g=None):
    n, d = x.shape
    single = pl.Buffered(1)
    in_specs = [
        pl.BlockSpec((tm, d), lambda i: (i, 0)),
        pl.BlockSpec((None, None, None, 3, 1, d), lambda i: (l, row_fn(i), sub, 0, 0, 0)),
        pl.BlockSpec((None, None, 1, d), lambda i: (l, sub, 0, 0)),
        pl.BlockSpec((None, None, d, D_FF), lambda i: (l, which, 0, 0), pipeline_mode=single),
        pl.BlockSpec((None, None, d, D_FF), lambda i: (l, which, 0, 0), pipeline_mode=single),
        pl.BlockSpec((None, None, D_FF, d), lambda i: (l, which, 0, 0), pipeline_mode=single),
    ]
    args = [x, mods, norm_sub, wg, wu, wd]
    if final_g is not None:
        in_specs.append(pl.BlockSpec((1, d), lambda i: (0, 0)))
        args.append(final_g.reshape(1, d))
    return pl.pallas_call(
        functools.partial(_ffn_kernel, final=final_g is not None),
        grid=(n // tm,),
        in_specs=in_specs,
        out_specs=pl.BlockSpec((tm, d), lambda i: (i, 0)),
        out_shape=jax.ShapeDtypeStruct((n, d), F32),
        compiler_params=_params(("arbitrary",)),
        name="ffn",
    )(*args)


def _inproj_kernel(x_ref, mod_ref, g_ref, w_ref, qn_ref, kn_ref, *rest, rope):
    if rope:
        cos_ref, sin_ref, q_ref, k_ref, v_ref, p_ref, u_ref = rest
    else:
        q_ref, k_ref, v_ref, p_ref, u_ref = rest
    h = _modulate(x_ref[...], g_ref[...], mod_ref[0], mod_ref[1]).astype(BF16)
    proj = _dot(h, w_ref[...])

    lane = lax.broadcasted_iota(jnp.int32, (1, LANES), 1)
    low = lane < HEAD_DIM
    first_half = (lane & (AXIS_DIM - 1)) < AXIS_FREQS
    er = lax.broadcasted_iota(jnp.int32, (LANES, LANES), 0) // HEAD_DIM
    ec = lax.broadcasted_iota(jnp.int32, (LANES, LANES), 1) // HEAD_DIM
    same_head = (er == ec).astype(BF16)

    def head_norm(blk, w):
        sq = blk * blk
        hi = sq.astype(BF16)
        lo = (sq - hi.astype(F32)).astype(BF16)
        ss = _dot(hi, same_head) + _dot(lo, same_head)
        return blk * lax.rsqrt(ss * (1.0 / HEAD_DIM) + EPS) * w

    def rotary(y):
        if not rope:
            return y
        partner = jnp.where(first_half,
                            pltpu.roll(y, LANES - AXIS_FREQS, 1),
                            pltpu.roll(y, AXIS_FREQS, 1))
        return y * cos_ref[...] + partner * sin_ref[...]

    zero = jnp.zeros((), F32)
    for j in range(N_Q_HEADS // 2):
        blk = proj[:, j * LANES:(j + 1) * LANES]
        y = rotary(head_norm(blk, qn_ref[...])) * (HEAD_DIM ** -0.5)
        swapped = pltpu.roll(y, HEAD_DIM, 1)
        if j < N_Q_HEADS // 4:
            even, odd = jnp.where(low, y, zero), jnp.where(low, swapped, zero)
        else:
            even, odd = jnp.where(low, zero, swapped), jnp.where(low, zero, y)
        q_ref[:, (2 * j) * LANES:(2 * j + 1) * LANES] = even.astype(BF16)
        q_ref[:, (2 * j + 1) * LANES:(2 * j + 2) * LANES] = odd.astype(BF16)

    kblk = proj[:, ATTN_W:ATTN_W + KV_W]
    k_ref[...] = rotary(head_norm(kblk, kn_ref[...])).astype(BF16)

    vblk = proj[:, ATTN_W + KV_W:ATTN_W + 2 * KV_W]
    vswap = pltpu.roll(vblk, HEAD_DIM, 1)
    v_ref[:, 0 * LANES:1 * LANES] = jnp.where(low, vblk, zero).astype(BF16)
    v_ref[:, 1 * LANES:2 * LANES] = jnp.where(low, zero, vswap).astype(BF16)
    v_ref[:, 2 * LANES:3 * LANES] = jnp.where(low, vswap, zero).astype(BF16)
    v_ref[:, 3 * LANES:4 * LANES] = jnp.where(low, zero, vblk).astype(BF16)

    p_ref[...] = proj[:, ATTN_W + 2 * KV_W:ATTN_W + 2 * KV_W + POOL_W]
    u_ref[...] = proj[:, ATTN_W + 2 * KV_W + POOL_W:]


def _inproj(x, mods, norm_sub, w_in, qn, kn, l, row_fn, tm, batch, rope_tabs):
    n, d = x.shape
    seg = n // batch
    tpb = seg // tm
    rope = rope_tabs is not None
    in_specs = [
        pl.BlockSpec((tm, d), lambda i: (i, 0)),
        pl.BlockSpec((None, None, None, 3, 1, d), lambda i: (l, row_fn(i), 1, 0, 0, 0)),
        pl.BlockSpec((None, None, 1, d), lambda i: (l, 1, 0, 0)),
        pl.BlockSpec((None, d, IN_W), lambda i: (l, 0, 0)),
        pl.BlockSpec((None, 1, LANES), lambda i: (l, 0, 0)),
        pl.BlockSpec((None, 1, LANES), lambda i: (l, 0, 0)),
    ]
    args = [x, mods, norm_sub, w_in, qn, kn]
    if rope:
        in_specs += [pl.BlockSpec((tm, LANES), lambda i: (i % tpb, 0))] * 2
        args += list(rope_tabs)
    out_shape = (
        jax.ShapeDtypeStruct((n, N_Q_HEADS * LANES), BF16),
        jax.ShapeDtypeStruct((n, KV_W), BF16),
        jax.ShapeDtypeStruct((n, 4 * LANES), BF16),
        jax.ShapeDtypeStruct((n, POOL_W), F32),
        jax.ShapeDtypeStruct((seg, batch * S5_W), F32),
    )
    out_specs = (
        pl.BlockSpec((tm, N_Q_HEADS * LANES), lambda i: (i, 0)),
        pl.BlockSpec((tm, KV_W), lambda i: (i, 0)),
        pl.BlockSpec((tm, 4 * LANES), lambda i: (i, 0)),
        pl.BlockSpec((tm, POOL_W), lambda i: (i, 0)),
        pl.BlockSpec((tm, S5_W), lambda i: (i % tpb, i // tpb)),
    )
    return pl.pallas_call(
        functools.partial(_inproj_kernel, rope=rope),
        grid=(n // tm,),
        in_specs=in_specs,
        out_specs=out_specs,
        out_shape=out_shape,
        compiler_params=_params(("arbitrary",)),
        name="inproj",
    )(*args)


def _attn_kernel(q_ref, k_ref, v_ref, o_ref, m_ref, l_ref, acc_ref):
    j = pl.program_id(2)

    @pl.when(j == 0)
    def _():
        m_ref[...] = jnp.full(m_ref.shape, -1e30, F32)
        l_ref[...] = jnp.zeros(l_ref.shape, F32)
        acc_ref[...] = jnp.zeros(acc_ref.shape, F32)

    k = k_ref[...]
    for h in range(N_Q_HEADS):
        q = q_ref[:, h * LANES:(h + 1) * LANES]
        s = lax.dot_general(q, k, (((1,), (1,)), ((), ())), preferred_element_type=F32)
        m_prev = m_ref[h]
        m_new = jnp.maximum(m_prev, jnp.max(s, axis=-1, keepdims=True))
        alpha = jnp.exp(m_prev - m_new)
        p = jnp.exp(s - m_new)
        l_ref[h] = alpha * l_ref[h] + jnp.sum(p, axis=-1, keepdims=True)
        vi = (h // Q_PER_KV) * 2 + (h % 2)
        acc_ref[h] = alpha * acc_ref[h] + _dot(p.astype(BF16), v_ref[:, vi * LANES:(vi + 1) * LANES])
        m_ref[h] = m_new

    @pl.when(j == pl.num_programs(2) - 1)
    def _():
        for jp in range(N_Q_HEADS // 2):
            o_ref[:, jp * LANES:(jp + 1) * LANES] = (
                acc_ref[2 * jp] / l_ref[2 * jp] + acc_ref[2 * jp + 1] / l_ref[2 * jp + 1])


def _attention(q, k, v, batch, tq, tk):
    nq = q.shape[0] // batch // tq
    nk = k.shape[0] // batch // tk
    return pl.pallas_call(
        _attn_kernel,
        grid=(batch, nq, nk),
        in_specs=[
            pl.BlockSpec((tq, N_Q_HEADS * LANES), lambda b, i, j: (b * nq + i, 0)),
            pl.BlockSpec((tk, KV_W), lambda b, i, j: (b * nk + j, 0)),
            pl.BlockSpec((tk, 4 * LANES), lambda b, i, j: (b * nk + j, 0)),
        ],
        out_specs=pl.BlockSpec((tq, ATTN_W), lambda b, i, j: (b * nq + i, 0)),
        out_shape=jax.ShapeDtypeStruct((q.shape[0], ATTN_W), F32),
        scratch_shapes=[
            pltpu.VMEM((N_Q_HEADS, tq, 1), F32),
            pltpu.VMEM((N_Q_HEADS, tq, 1), F32),
            pltpu.VMEM((N_Q_HEADS, tq, LANES), F32),
        ],
        compiler_params=_params(("arbitrary", "arbitrary", "arbitrary")),
        name="attention",
    )(q, k, v)


def _s5_kernel(uf_ref, ub_ref, a_ref, bblk_ref, cblk_ref, yf_ref, yb_ref, sf_ref, sb_ref, st_ref, *, tc, batch):
    nch = S5_CHANNELS

    @pl.when(pl.program_id(0) == 0)
    def _():
        st_ref[...] = jnp.zeros(st_ref.shape, F32)

    ngrp = tc * batch // SUBLANES
    steps = SUBLANES // batch
    sf_ref[...] = _dot(uf_ref[...].astype(BF16), bblk_ref[0]).reshape(ngrp, SUBLANES, 2 * nch)
    sb_ref[...] = _dot(ub_ref[...].astype(BF16), bblk_ref[1]).reshape(ngrp, SUBLANES, 2 * nch)
    af_re, af_im = a_ref[0, 0], a_ref[0, 1]
    ab_re, ab_im = a_ref[1, 0], a_ref[1, 1]

    def advance(ref, g, r, a_re, a_im, s_re, s_im):
        rows = pl.ds(r * batch, batch)
        n_re = a_re * s_re - a_im * s_im + ref[g, rows, pl.ds(0, nch)]
        n_im = a_re * s_im + a_im * s_re + ref[g, rows, pl.ds(nch, nch)]
        ref[g, rows, pl.ds(0, nch)] = n_re
        ref[g, rows, pl.ds(nch, nch)] = n_im
        return n_re, n_im

    def body(g, carry):
        f_re, f_im, b_re, b_im = carry
        gb = ngrp - 1 - g
        for r in range(steps):
            f_re, f_im = advance(sf_ref, g, r, af_re, af_im, f_re, f_im)
            b_re, b_im = advance(sb_ref, gb, steps - 1 - r, ab_re, ab_im, b_re, b_im)
        return f_re, f_im, b_re, b_im

    carry = lax.fori_loop(0, ngrp, body, (st_ref[0], st_ref[1], st_ref[2], st_ref[3]))
    for idx in range(4):
        st_ref[idx] = carry[idx]
    rows = ngrp * SUBLANES
    yf_ref[...] = _dot(sf_ref[...].reshape(rows, 2 * nch).astype(BF16), cblk_ref[0])
    yb_ref[...] = _dot(sb_ref[...].reshape(rows, 2 * nch).astype(BF16), cblk_ref[1])


def _s5_scan(u_tb, avec, bblk, cblk, l, batch, n_lat_tiles, n_ctx_tiles, tc):
    rows = tc * batch
    n_tiles = n_lat_tiles + n_ctx_tiles
    nch = S5_CHANNELS

    def fwd(i):
        return jnp.where(i < n_ctx_tiles, n_lat_tiles + i, i - n_ctx_tiles)

    def bwd(i):
        return n_tiles - 1 - i

    shp = jax.ShapeDtypeStruct(u_tb.shape, F32)
    return pl.pallas_call(
        functools.partial(_s5_kernel, tc=tc, batch=batch),
        grid=(n_tiles,),
        in_specs=[
            pl.BlockSpec((rows, S5_W), lambda i: (fwd(i), 0)),
            pl.BlockSpec((rows, S5_W), lambda i: (bwd(i), 0)),
            pl.BlockSpec((None, 2, 2, 1, nch), lambda i: (l, 0, 0, 0, 0)),
            pl.BlockSpec((None, 2, S5_W, 2 * nch), lambda i: (l, 0, 0, 0)),
            pl.BlockSpec((None, 2, 2 * nch, S5_W), lambda i: (l, 0, 0, 0)),
        ],
        out_specs=(
            pl.BlockSpec((rows, S5_W), lambda i: (fwd(i), 0)),
            pl.BlockSpec((rows, S5_W), lambda i: (bwd(i), 0)),
        ),
        out_shape=(shp, shp),
        scratch_shapes=[
            pltpu.VMEM((rows // SUBLANES, SUBLANES, 2 * nch), F32),
            pltpu.VMEM((rows // SUBLANES, SUBLANES, 2 * nch), F32),
            pltpu.VMEM((4, batch, nch), F32),
        ],
        compiler_params=_params(("arbitrary",)),
        name="s5_scan",
    )(u_tb, u_tb, avec, bblk, cblk)


def _gelu_tanh(x):
    return 0.5 * x * (1.0 + jnp.tanh(math.sqrt(2.0 / math.pi) * (x + 0.044715 * (x * x * x))))


def _merge_kernel(x_ref, mod_ref, attn_ref, pin_ref, prev_ref, next_ref, u_ref, yf_ref, yb_ref,
                  an_ref, sn_ref, pw_ref, ps_ref, dsk_ref, gw_ref, gb_ref, wo_ref, o_ref, ext_ref,
                  *, tm, tpb, seg):
    ti = pl.program_id(0) % tpb
    a_n = _rmsnorm(attn_ref[...], an_ref[...])

    hal = POOL_HALO
    ext_ref[pl.ds(0, hal), :] = jnp.where(ti > 0, prev_ref[...], 0.0)
    ext_ref[pl.ds(hal, tm), :] = pin_ref[...]
    ext_ref[pl.ds(hal + tm, hal), :] = jnp.where(ti < tpb - 1, next_ref[...], 0.0)
    t = ti * tm + lax.broadcasted_iota(jnp.int32, (tm, 1), 0)
    low = lax.broadcasted_iota(jnp.int32, (1, LANES), 1) < POOL_CH

    def shifted(o, c):
        return ext_ref[pl.ds(hal + o, tm), pl.ds(c * LANES, LANES)]

    def count(w):
        hi = jnp.minimum(t + (w - w // 2), seg)
        lo = jnp.maximum(t - w // 2, 0)
        return (hi - lo).astype(F32)

    def window_sums(c, w_small, w_big):
        acc = shifted(0, c)
        for o in range(-(w_small // 2), w_small - w_small // 2):
            if o != 0:
                acc = acc + shifted(o, c)
        small = acc
        for o in range(-(w_big // 2), w_big - w_big // 2):
            if not (-(w_small // 2) <= o < w_small - w_small // 2):
                acc = acc + shifted(o, c)
        return jnp.where(low, small / count(w_small), acc / count(w_big))

    pooled = jnp.concatenate(
        [window_sums(0, POOL_WINDOWS[0], POOL_WINDOWS[1]),
         window_sums(1, POOL_WINDOWS[2], POOL_WINDOWS[3])], axis=-1) - pin_ref[...]
    p_out = _dot(pooled.astype(BF16), pw_ref[...]) * ps_ref[...]

    y = dsk_ref[...] * u_ref[...] + yf_ref[...] + yb_ref[...]
    g = _gelu_tanh(y)
    glu = g * jax.nn.sigmoid(_dot(g.astype(BF16), gw_ref[...]) + gb_ref[...])
    s_n = _rmsnorm(glu, sn_ref[...])

    cat = jnp.concatenate([a_n, p_out, s_n], axis=-1).astype(BF16)
    o_ref[...] = x_ref[...] + mod_ref[2] * _dot(cat, wo_ref[...])


def _merge(x, mods, attn, pin, u_tm, yf_tm, yb_tm, y_row_off, an, sn, pw, ps, dsk, gw, gb, wo,
           l, row_fn, tm, batch):
    n, d = x.shape
    seg = n // batch
    tpb = seg // tm
    hal = POOL_HALO
    nhb = n // hal
    yo = y_row_off // tm

    def tmaj(off):
        return pl.BlockSpec((tm, S5_W), lambda i: (off + i % tpb, i // tpb))

    def vec(width):
        return pl.BlockSpec((None, 1, width), lambda i: (l, 0, 0))

    def mat(r, c):
        return pl.BlockSpec((None, r, c), lambda i: (l, 0, 0))

    in_specs = [
        pl.BlockSpec((tm, d), lambda i: (i, 0)),
        pl.BlockSpec((None, None, None, 3, 1, d), lambda i: (l, row_fn(i), 1, 0, 0, 0)),
        pl.BlockSpec((tm, ATTN_W), lambda i: (i, 0)),
        pl.BlockSpec((tm, POOL_W), lambda i: (i, 0)),
        pl.BlockSpec((hal, POOL_W), lambda i: (jnp.maximum(i * (tm // hal) - 1, 0), 0)),
        pl.BlockSpec((hal, POOL_W), lambda i: (jnp.minimum((i + 1) * (tm // hal), nhb - 1), 0)),
        tmaj(0), tmaj(yo), tmaj(yo),
        vec(ATTN_W), vec(S5_W), mat(POOL_W, POOL_W), vec(POOL_W), vec(S5_W), mat(S5_W, S5_W), vec(S5_W),
        mat(d, d),
    ]
    return pl.pallas_call(
        functools.partial(_merge_kernel, tm=tm, tpb=tpb, seg=seg),
        grid=(n // tm,),
        in_specs=in_specs,
        out_specs=pl.BlockSpec((tm, d), lambda i: (i, 0)),
        out_shape=jax.ShapeDtypeStruct((n, d), F32),
        scratch_shapes=[pltpu.VMEM((tm + 2 * hal, POOL_W), F32)],
        compiler_params=_params(("arbitrary",)),
        name="merge",
    )(x, mods, attn, pin, pin, pin, u_tm, yf_tm, yb_tm, an, sn, pw, ps, dsk, gw, gb, wo)


def _rope_tables(seq):
    rows = seq // GRID_W
    row = jnp.broadcast_to(jnp.arange(rows, dtype=F32)[:, None], (rows, GRID_W)).reshape(-1)
    col = jnp.broadcast_to(jnp.arange(GRID_W, dtype=F32)[None, :], (rows, GRID_W)).reshape(-1)
    inv = ROPE_THETA ** (-2.0 * jnp.arange(AXIS_FREQS, dtype=F32) / AXIS_DIM)
    ang = jnp.concatenate([row[:, None] * inv, col[:, None] * inv], axis=-1)
    cos, sin = jnp.cos(ang), jnp.sin(ang)
    dim = jnp.arange(LANES) % HEAD_DIM
    idx = (dim // AXIS_DIM) * AXIS_FREQS + dim % AXIS_FREQS
    sign = jnp.where((dim % AXIS_DIM) < AXIS_FREQS, -1.0, 1.0).astype(F32)
    return cos[:, idx], sin[:, idx] * sign


def _block_diag(w):
    dep, g, c, c2 = w.shape
    eye = jnp.eye(g, dtype=w.dtype)
    return jnp.einsum('gk,lgcd->lgckd', eye, w).reshape(dep, g * c, g * c2)


def kernel(x, c, ctx, c_ctx, w_ada, b_ada, norm_sub, w_ffn_gate, w_ffn_up, w_ffn_down, w_in, w_out, q_norm, k_norm, attn_out_norm, ssm_out_norm, pool_w, pool_scale, s5_lam_re, s5_lam_im, s5_log_dt, s5_b_re, s5_b_im, s5_c_re, s5_c_im, s5_d, s5_glu_w, s5_glu_b, final_norm):
    batch, seq, d = x.shape
    ctx_len = ctx.shape[1]
    tm_lat, tm_ctx = 512, 256
    tq, tk = 512, 256
    tc = 128
    lat_tpb = seq // tm_lat

    cond = jnp.concatenate([c_ctx[None], c, jnp.zeros((SUBLANES - 1 - batch, d), F32)], axis=0)
    mods = _adaln(cond, w_ada, b_ada).reshape(DEPTH, SUBLANES, 3, 3, 1, d)
    lat_row = lambda i: 1 + i // lat_tpb
    ctx_row = lambda i: 0

    nsub = norm_sub.reshape(DEPTH, 3, 1, d)
    wg, wu, wd = w_ffn_gate.astype(BF16), w_ffn_up.astype(BF16), w_ffn_down.astype(BF16)
    w_in_b, w_out_b = w_in.astype(BF16), w_out.astype(BF16)
    qn = jnp.tile(q_norm, (1, LANES // HEAD_DIM)).reshape(DEPTH, 1, LANES)
    kn = jnp.tile(k_norm, (1, LANES // HEAD_DIM)).reshape(DEPTH, 1, LANES)
    an = attn_out_norm.reshape(DEPTH, 1, ATTN_W)
    sn = ssm_out_norm.reshape(DEPTH, 1, S5_W)
    pw = _block_diag(pool_w).astype(BF16)
    ps = pool_scale.reshape(DEPTH, 1, POOL_W)
    dsk = s5_d.reshape(DEPTH, 1, S5_W)
    gw = _block_diag(s5_glu_w).astype(BF16)
    gb = s5_glu_b.reshape(DEPTH, 1, S5_W)
    avec, bblk, cblk = _s5_tables(s5_lam_re, s5_lam_im, s5_log_dt, s5_b_re, s5_b_im, s5_c_re, s5_c_im)
    rope_tabs = _rope_tables(seq)

    xs = x.reshape(batch * seq, d)
    zs = ctx.reshape(batch * ctx_len, d)
    for l in range(DEPTH):
        need_ctx_out = l < DEPTH - 1
        xs = _ffn(xs, mods, nsub, wg, wu, wd, l, 0, 0, lat_row, tm_lat)
        zs = _ffn(zs, mods, nsub, wg, wu, wd, l, 0, 0, ctx_row, tm_ctx)

        qx, kx, vx, px, ux = _inproj(xs, mods, nsub, w_in_b, qn, kn, l, lat_row, tm_lat, batch, rope_tabs)
        qz, kz, vz, pz, uz = _inproj(zs, mods, nsub, w_in_b, qn, kn, l, ctx_row, tm_ctx, batch, None)

        k_all = jnp.concatenate([kz.reshape(batch, ctx_len, -1), kx.reshape(batch, seq, -1)], axis=1)
        v_all = jnp.concatenate([vz.reshape(batch, ctx_len, -1), vx.reshape(batch, seq, -1)], axis=1)
        attn_x = _attention(qx, k_all.reshape(batch * (ctx_len + seq), -1),
                            v_all.reshape(batch * (ctx_len + seq), -1), batch, tq, tk)

        u_tb = jnp.concatenate([ux, uz], axis=0).reshape((seq + ctx_len) * batch, S5_W)
        yf, yb = _s5_scan(u_tb, avec, bblk, cblk, l, batch, seq // tc, ctx_len // tc, tc)
        yf = yf.reshape(seq + ctx_len, batch * S5_W)
        yb = yb.reshape(seq + ctx_len, batch * S5_W)

        xs = _merge(xs, mods, attn_x, px, ux, yf, yb, 0, an, sn, pw, ps, dsk, gw, gb, w_out_b,
                    l, lat_row, tm_lat, batch)
        last = l == DEPTH - 1
        xs = _ffn(xs, mods, nsub, wg, wu, wd, l, 2, 1, lat_row, tm_lat,
                  final_g=final_norm if last else None)
        if need_ctx_out:
            attn_z = _attention(qz, kz, vz, batch, tm_ctx, tm_ctx)
            zs = _merge(zs, mods, attn_z, pz, uz, yf, yb, seq, an, sn, pw, ps, dsk, gw, gb, w_out_b,
                        l, ctx_row, tm_ctx, batch)
            zs = _ffn(zs, mods, nsub, wg, wu, wd, l, 2, 1, ctx_row, tm_ctx)
    return xs.reshape(batch, seq, d)
```

```python
import functools
import math

import jax
import jax.numpy as jnp
from jax import lax
from jax.experimental import pallas as pl
from jax.experimental.pallas import tpu as pltpu

F32 = jnp.float32
BF16 = jnp.bfloat16

D_MODEL = 1024
DEPTH = 4
GRID_W = 64
N_MOD = 9
EPS = 1e-6
D_FF = 2816
ATTN_W = 512
POOL_W = 256
S5_W = 256
HEAD_DIM = 64
N_Q_HEADS = 8
N_KV_HEADS = 2
Q_PER_KV = 4
KV_W = 128
ROPE_THETA = 10000.0
AXIS_DIM = 32
AXIS_FREQS = 16
POOL_WINDOWS = (2, 4, 8, 16)
POOL_CH = 64
S5_CH = 16
S5_GROUPS = 16
S5_STATE = 64
S5_CHANNELS = S5_GROUPS * S5_STATE
IN_W = ATTN_W + 2 * KV_W + POOL_W + S5_W

LANES = 128
SUBLANES = 8
POOL_HALO = 8
KV_CHUNK = 256
V_ROWS = HEAD_DIM + 16
VMEM_LIMIT = 56 * 1024 * 1024


def _params(sem):
    return pltpu.CompilerParams(dimension_semantics=sem, vmem_limit_bytes=VMEM_LIMIT)


def _dot(a, b):
    return jnp.dot(a, b, preferred_element_type=F32)


def _silu(x):
    return x * jax.nn.sigmoid(x)


def _rmsnorm(x, g):
    return x * lax.rsqrt(jnp.mean(x * x, axis=-1, keepdims=True) + EPS) * g


def _modulate(x, g, shift, scale):
    return _rmsnorm(x, g) * (1.0 + scale) + shift


def _adaln_kernel(cond_ref, w_ref, b_ref, o_ref):
    s = _silu(cond_ref[...]).astype(BF16)
    o_ref[...] = _dot(s, w_ref[...].astype(BF16)) + b_ref[...]


def _adaln(cond, w_ada, b_ada):
    d = D_MODEL
    tn = 1024
    nj = (N_MOD * d) // tn
    return pl.pallas_call(
        _adaln_kernel,
        grid=(DEPTH, nj),
        in_specs=[
            pl.BlockSpec((SUBLANES, d), lambda l, j: (0, 0)),
            pl.BlockSpec((None, d, tn), lambda l, j: (l, 0, j)),
            pl.BlockSpec((None, 1, tn), lambda l, j: (l, 0, j)),
        ],
        out_specs=pl.BlockSpec((None, SUBLANES, tn), lambda l, j: (l, 0, j)),
        out_shape=jax.ShapeDtypeStruct((DEPTH, SUBLANES, N_MOD * d), F32),
        compiler_params=_params(("arbitrary", "arbitrary")),
        name="adaln",
    )(cond, w_ada, b_ada.reshape(DEPTH, 1, N_MOD * d))


def _zoh_kernel(lr_ref, li_ref, ldt_ref, br_ref, bi_ref, abr_ref, abi_ref, bbr_ref, bbi_ref):
    dt = jnp.exp(ldt_ref[...])
    lr = lr_ref[...]
    li = li_ref[...]
    mag = jnp.exp(lr * dt)
    ab_re = mag * jnp.cos(li * dt)
    ab_im = mag * jnp.sin(li * dt)
    den = lr * lr + li * li
    f_re = ((ab_re - 1.0) * lr + ab_im * li) / den
    f_im = (ab_im * lr - (ab_re - 1.0) * li) / den
    br = br_ref[...]
    bi = bi_ref[...]
    abr_ref[...] = ab_re
    abi_ref[...] = ab_im
    bbr_ref[...] = f_re * br - f_im * bi
    bbi_ref[...] = f_re * bi + f_im * br


def _s5_tables(lam_re, lam_im, log_dt, b_re, b_im, c_re, c_im):
    g, h, p = S5_GROUPS, S5_CH, S5_STATE
    rows = DEPTH * 2 * g * h

    def rep(a):
        return jnp.broadcast_to(a[:, :, :, None, :], (DEPTH, 2, g, h, p)).reshape(rows, p)

    ldt = jnp.broadcast_to(log_dt[:, :, :, None, None], (DEPTH, 2, g, h, p)).reshape(rows, p)
    bt_re = jnp.swapaxes(b_re, -1, -2).reshape(rows, p)
    bt_im = jnp.swapaxes(b_im, -1, -2).reshape(rows, p)
    shp = jax.ShapeDtypeStruct((rows, p), F32)
    ab_re, ab_im, bb_re, bb_im = pl.pallas_call(
        _zoh_kernel, out_shape=(shp, shp, shp, shp), name="s5_zoh",
    )(rep(lam_re), rep(lam_im), ldt, bt_re, bt_im)

    def chan(a):
        return a.reshape(DEPTH, 2, g, h, p)[:, :, :, 0, :].reshape(DEPTH, 2, g * p)

    avec = jnp.stack([chan(ab_re), chan(ab_im)], axis=2).reshape(DEPTH, 2, 2, 1, g * p)
    eye = jnp.eye(g, dtype=F32)

    def in_blk(bt):
        bt = bt.reshape(DEPTH, 2, g, h, p)
        return jnp.einsum('gk,ldghp->ldghkp', eye, bt).reshape(DEPTH, 2, g * h, g * p)

    def out_blk(cc):
        return jnp.einsum('gk,ldkhp->ldgpkh', eye, cc).reshape(DEPTH, 2, g * p, g * h)

    bblk = jnp.concatenate([in_blk(bb_re), in_blk(bb_im)], axis=-1).astype(BF16)
    cblk = jnp.concatenate([out_blk(c_re), out_blk(-c_im)], axis=-2).astype(BF16)
    return avec, bblk, cblk


def _ffn_kernel(x_ref, mod_ref, g_ref, wg_ref, wu_ref, wd_ref, *rest, final):
    if final:
        fg_ref, o_ref = rest
    else:
        (o_ref,) = rest
    x = x_ref[...]
    h = _modulate(x, g_ref[...], mod_ref[0], mod_ref[1]).astype(BF16)
    a = _dot(h, wg_ref[...])
    u = _dot(h, wu_ref[...])
    act = (_silu(a) * u).astype(BF16)
    y = x + (0.5 * mod_ref[2]) * _dot(act, wd_ref[...])
    if final:
        y = _rmsnorm(y, fg_ref[...])
    o_ref[...] = y


def _ffn(x, mods, norm_sub, wg, wu, wd, l, sub, which, row_fn, tm, final_g=None):
    n, d = x.shape
    single = pl.Buffered(1)
    in_specs = [
        pl.BlockSpec((tm, d), lambda i: (i, 0)),
        pl.BlockSpec((None, None, None, 3, 1, d), lambda i: (l, row_fn(i), sub, 0, 0, 0)),
        pl.BlockSpec((None, None, 1, d), lambda i: (l, sub, 0, 0)),
        pl.BlockSpec((None, None, d, D_FF), lambda i: (l, which, 0, 0), pipeline_mode=single),
        pl.BlockSpec((None, None, d, D_FF), lambda i: (l, which, 0, 0), pipeline_mode=single),
        pl.BlockSpec((None, None, D_FF, d), lambda i: (l, which, 0, 0), pipeline_mode=single),
    ]
    args = [x, mods, norm_sub, wg, wu, wd]
    if final_g is not None:
        in_specs.append(pl.BlockSpec((1, d), lambda i: (0, 0)))
        args.append(final_g.reshape(1, d))
    return pl.pallas_call(
        functools.partial(_ffn_kernel, final=final_g is not None),
        grid=(n // tm,),
        in_specs=in_specs,
        out_specs=pl.BlockSpec((tm, d), lambda i: (i, 0)),
        out_shape=jax.ShapeDtypeStruct((n, d), F32),
        compiler_params=_params(("arbitrary",)),
        name="ffn",
    )(*args)


def _inproj_kernel(x_ref, mod_ref, g_ref, w_ref, qn_ref, kn_ref, *rest, rope):
    if rope:
        cos_ref, sin_ref, q_ref, k_ref, v_ref, p_ref, u_ref = rest
    else:
        q_ref, k_ref, v_ref, p_ref, u_ref = rest
    h = _modulate(x_ref[...], g_ref[...], mod_ref[0], mod_ref[1]).astype(BF16)
    proj = _dot(h, w_ref[...])

    lane = lax.broadcasted_iota(jnp.int32, (1, LANES), 1)
    low = lane < HEAD_DIM
    first_half = (lane & (AXIS_DIM - 1)) < AXIS_FREQS
    er = lax.broadcasted_iota(jnp.int32, (LANES, LANES), 0) // HEAD_DIM
    ec = lax.broadcasted_iota(jnp.int32, (LANES, LANES), 1) // HEAD_DIM
    same_head = (er == ec).astype(BF16)

    def head_norm(blk, w):
        sq = blk * blk
        hi = sq.astype(BF16)
        lo = (sq - hi.astype(F32)).astype(BF16)
        ss = _dot(hi, same_head) + _dot(lo, same_head)
        return blk * lax.rsqrt(ss * (1.0 / HEAD_DIM) + EPS) * w

    def rotary(y):
        if not rope:
            return y
        partner = jnp.where(first_half,
                            pltpu.roll(y, LANES - AXIS_FREQS, 1),
                            pltpu.roll(y, AXIS_FREQS, 1))
        return y * cos_ref[...] + partner * sin_ref[...]

    zero = jnp.zeros((), F32)
    q_scale = HEAD_DIM ** -0.5 * math.log2(math.e)
    for j in range(N_Q_HEADS // 2):
        blk = proj[:, j * LANES:(j + 1) * LANES]
        y = rotary(head_norm(blk, qn_ref[...])) * q_scale
        swapped = pltpu.roll(y, HEAD_DIM, 1)
        if j < N_Q_HEADS // 4:
            even, odd = jnp.where(low, y, zero), jnp.where(low, swapped, zero)
        else:
            even, odd = jnp.where(low, zero, swapped), jnp.where(low, zero, y)
        q_ref[(2 * j) * LANES:(2 * j + 1) * LANES, :] = even.T.astype(BF16)
        q_ref[(2 * j + 1) * LANES:(2 * j + 2) * LANES, :] = odd.T.astype(BF16)

    kblk = proj[:, ATTN_W:ATTN_W + KV_W]
    kk = rotary(head_norm(kblk, kn_ref[...])).astype(BF16)
    vblk = proj[:, ATTN_W + KV_W:ATTN_W + 2 * KV_W]
    ones = jnp.ones((V_ROWS - HEAD_DIM, KV_CHUNK), BF16)
    for c in range(k_ref.shape[0]):
        k_ref[c] = kk[c * KV_CHUNK:(c + 1) * KV_CHUNK, :]
        vt = vblk[c * KV_CHUNK:(c + 1) * KV_CHUNK, :].T.astype(BF16)
        for g in range(N_KV_HEADS):
            v_ref[c, g, 0:HEAD_DIM, :] = vt[g * HEAD_DIM:(g + 1) * HEAD_DIM, :]
            v_ref[c, g, HEAD_DIM:V_ROWS, :] = ones

    p_ref[...] = proj[:, ATTN_W + 2 * KV_W:ATTN_W + 2 * KV_W + POOL_W]
    u_ref[...] = proj[:, ATTN_W + 2 * KV_W + POOL_W:]


def _inproj(x, mods, norm_sub, w_in, qn, kn, l, row_fn, tm, batch, rope_tabs):
    n, d = x.shape
    seg = n // batch
    tpb = seg // tm
    rope = rope_tabs is not None
    in_specs = [
        pl.BlockSpec((tm, d), lambda i: (i, 0)),
        pl.BlockSpec((None, None, None, 3, 1, d), lambda i: (l, row_fn(i), 1, 0, 0, 0)),
        pl.BlockSpec((None, None, 1, d), lambda i: (l, 1, 0, 0)),
        pl.BlockSpec((None, d, IN_W), lambda i: (l, 0, 0)),
        pl.BlockSpec((None, 1, LANES), lambda i: (l, 0, 0)),
        pl.BlockSpec((None, 1, LANES), lambda i: (l, 0, 0)),
    ]
    args = [x, mods, norm_sub, w_in, qn, kn]
    if rope:
        in_specs += [pl.BlockSpec((tm, LANES), lambda i: (i % tpb, 0))] * 2
        args += list(rope_tabs)
    cpt = tm // KV_CHUNK
    out_shape = (
        jax.ShapeDtypeStruct((N_Q_HEADS * LANES, n), BF16),
        jax.ShapeDtypeStruct((n // KV_CHUNK, KV_CHUNK, KV_W), BF16),
        jax.ShapeDtypeStruct((n // KV_CHUNK, N_KV_HEADS, V_ROWS, KV_CHUNK), BF16),
        jax.ShapeDtypeStruct((n, POOL_W), F32),
        jax.ShapeDtypeStruct((seg, batch * S5_W), F32),
    )
    out_specs = (
        pl.BlockSpec((N_Q_HEADS * LANES, tm), lambda i: (0, i)),
        pl.BlockSpec((cpt, KV_CHUNK, KV_W), lambda i: (i, 0, 0)),
        pl.BlockSpec((cpt, N_KV_HEADS, V_ROWS, KV_CHUNK), lambda i: (i, 0, 0, 0)),
        pl.BlockSpec((tm, POOL_W), lambda i: (i, 0)),
        pl.BlockSpec((tm, S5_W), lambda i: (i % tpb, i // tpb)),
    )
    return pl.pallas_call(
        functools.partial(_inproj_kernel, rope=rope),
        grid=(n // tm,),
        in_specs=in_specs,
        out_specs=out_specs,
        out_shape=out_shape,
        compiler_params=_params(("arbitrary",)),
        name="inproj",
    )(*args)


def _attn_kernel(q_ref, k_ref, v_ref, o_ref, m_ref, acc_ref, ot_ref, sa_ref, sb_ref):
    nk = k_ref.shape[0]
    m_ref[...] = jnp.full(m_ref.shape, -1e30, F32)
    acc_ref[...] = jnp.zeros(acc_ref.shape, F32)

    def scores(kj, h):
        return _dot(kj, q_ref[h * LANES:(h + 1) * LANES, :])

    def step(j, cur_ref, nxt_ref):
        kn = None if nxt_ref is None else k_ref[j + 1]
        for h in range(N_Q_HEADS):
            if nxt_ref is not None:
                nxt_ref[h] = scores(kn, h)
            st = cur_ref[h]
            m_prev = m_ref[h]
            m_new = jnp.maximum(m_prev, jnp.max(st, axis=0, keepdims=True))
            alpha = jnp.exp2(m_prev - m_new)
            pt = jnp.exp2(st - m_new).astype(BF16)
            m_ref[h] = m_new
            acc_ref[h] = alpha * acc_ref[h] + _dot(v_ref[j, h // Q_PER_KV], pt)

    k0 = k_ref[0]
    for h in range(N_Q_HEADS):
        sa_ref[h] = scores(k0, h)

    def body(i, carry):
        step(2 * i, sa_ref, sb_ref)
        step(2 * i + 1, sb_ref, sa_ref)
        return carry

    pairs = (nk - 1) // 2
    lax.fori_loop(0, pairs, body, 0)
    if nk - 1 - 2 * pairs:
        step(nk - 2, sa_ref, sb_ref)
        step(nk - 1, sb_ref, None)
    else:
        step(nk - 1, sa_ref, None)
    for h in range(N_Q_HEADS):
        ot_ref[h * HEAD_DIM:(h + 1) * HEAD_DIM, :] = (
            acc_ref[h, 0:HEAD_DIM, :] / acc_ref[h, HEAD_DIM:HEAD_DIM + 1, :])
    o_ref[...] = ot_ref[...].T


def _attention(qt, k, vt, batch, tq):
    n = qt.shape[1]
    nq = n // batch // tq
    nk = k.shape[0] // batch
    return pl.pallas_call(
        _attn_kernel,
        grid=(batch, nq),
        in_specs=[
            pl.BlockSpec((N_Q_HEADS * LANES, tq), lambda b, i: (0, b * nq + i)),
            pl.BlockSpec((nk, KV_CHUNK, KV_W), lambda b, i: (b, 0, 0)),
            pl.BlockSpec((nk, N_KV_HEADS, V_ROWS, KV_CHUNK), lambda b, i: (b, 0, 0, 0)),
        ],
        out_specs=pl.BlockSpec((tq, ATTN_W), lambda b, i: (b * nq + i, 0)),
        out_shape=jax.ShapeDtypeStruct((n, ATTN_W), F32),
        scratch_shapes=[
            pltpu.VMEM((N_Q_HEADS, 1, tq), F32),
            pltpu.VMEM((N_Q_HEADS, V_ROWS, tq), F32),
            pltpu.VMEM((ATTN_W, tq), F32),
            pltpu.VMEM((N_Q_HEADS, KV_CHUNK, tq), F32),
            pltpu.VMEM((N_Q_HEADS, KV_CHUNK, tq), F32),
        ],
        compiler_params=_params(("arbitrary", "arbitrary")),
        name="attention",
    )(qt, k, vt)


def _s5_kernel(uf_ref, ub_ref, a_ref, bblk_ref, cblk_ref, yf_ref, yb_ref, sf_ref, sb_ref, st_ref, *, tc, batch):
    nch = S5_CHANNELS

    @pl.when(pl.program_id(0) == 0)
    def _():
        st_ref[...] = jnp.zeros(st_ref.shape, F32)

    ngrp = tc * batch // SUBLANES
    steps = SUBLANES // batch
    sf_ref[...] = _dot(uf_ref[...].astype(BF16), bblk_ref[0]).reshape(ngrp, SUBLANES, 2 * nch)
    sb_ref[...] = _dot(ub_ref[...].astype(BF16), bblk_ref[1]).reshape(ngrp, SUBLANES, 2 * nch)
    af_re, af_im = a_ref[0, 0], a_ref[0, 1]
    ab_re, ab_im = a_ref[1, 0], a_ref[1, 1]

    def advance(ref, g, r, a_re, a_im, s_re, s_im):
        rows = pl.ds(r * batch, batch)
        n_re = a_re * s_re - a_im * s_im + ref[g, rows, pl.ds(0, nch)]
        n_im = a_re * s_im + a_im * s_re + ref[g, rows, pl.ds(nch, nch)]
        ref[g, rows, pl.ds(0, nch)] = n_re
        ref[g, rows, pl.ds(nch, nch)] = n_im
        return n_re, n_im

    def body(g, carry):
        f_re, f_im, b_re, b_im = carry
        gb = ngrp - 1 - g
        for r in range(steps):
            f_re, f_im = advance(sf_ref, g, r, af_re, af_im, f_re, f_im)
            b_re, b_im = advance(sb_ref, gb, steps - 1 - r, ab_re, ab_im, b_re, b_im)
        return f_re, f_im, b_re, b_im

    carry = lax.fori_loop(0, ngrp, body, (st_ref[0], st_ref[1], st_ref[2], st_ref[3]))
    for idx in range(4):
        st_ref[idx] = carry[idx]
    rows = ngrp * SUBLANES
    yf_ref[...] = _dot(sf_ref[...].reshape(rows, 2 * nch).astype(BF16), cblk_ref[0])
    yb_ref[...] = _dot(sb_ref[...].reshape(rows, 2 * nch).astype(BF16), cblk_ref[1])


def _s5_scan(u_tb, avec, bblk, cblk, l, batch, n_lat_tiles, n_ctx_tiles, tc):
    rows = tc * batch
    n_tiles = n_lat_tiles + n_ctx_tiles
    nch = S5_CHANNELS

    def fwd(i):
        return jnp.where(i < n_ctx_tiles, n_lat_tiles + i, i - n_ctx_tiles)

    def bwd(i):
        return n_tiles - 1 - i

    shp = jax.ShapeDtypeStruct(u_tb.shape, F32)
    return pl.pallas_call(
        functools.partial(_s5_kernel, tc=tc, batch=batch),
        grid=(n_tiles,),
        in_specs=[
            pl.BlockSpec((rows, S5_W), lambda i: (fwd(i), 0)),
            pl.BlockSpec((rows, S5_W), lambda i: (bwd(i), 0)),
            pl.BlockSpec((None, 2, 2, 1, nch), lambda i: (l, 0, 0, 0, 0)),
            pl.BlockSpec((None, 2, S5_W, 2 * nch), lambda i: (l, 0, 0, 0)),
            pl.BlockSpec((None, 2, 2 * nch, S5_W), lambda i: (l, 0, 0, 0)),
        ],
        out_specs=(
            pl.BlockSpec((rows, S5_W), lambda i: (fwd(i), 0)),
            pl.BlockSpec((rows, S5_W), lambda i: (bwd(i), 0)),
        ),
        out_shape=(shp, shp),
        scratch_shapes=[
            pltpu.VMEM((rows // SUBLANES, SUBLANES, 2 * nch), F32),
            pltpu.VMEM((rows // SUBLANES, SUBLANES, 2 * nch), F32),
            pltpu.VMEM((4, batch, nch), F32),
        ],
        compiler_params=_params(("arbitrary",)),
        name="s5_scan",
    )(u_tb, u_tb, avec, bblk, cblk)


def _gelu_tanh(x):
    return 0.5 * x * (1.0 + jnp.tanh(math.sqrt(2.0 / math.pi) * (x + 0.044715 * (x * x * x))))


def _merge_kernel(x_ref, mod_ref, attn_ref, pin_ref, prev_ref, next_ref, u_ref, yf_ref, yb_ref,
                  an_ref, sn_ref, pw_ref, ps_ref, dsk_ref, gw_ref, gb_ref, wo_ref, o_ref, ext_ref,
                  *, tm, tpb, seg):
    ti = pl.program_id(0) % tpb
    a_n = _rmsnorm(attn_ref[...], an_ref[...])

    hal = POOL_HALO
    ext_ref[pl.ds(0, hal), :] = jnp.where(ti > 0, prev_ref[...], 0.0)
    ext_ref[pl.ds(hal, tm), :] = pin_ref[...]
    ext_ref[pl.ds(hal + tm, hal), :] = jnp.where(ti < tpb - 1, next_ref[...], 0.0)
    t = ti * tm + lax.broadcasted_iota(jnp.int32, (tm, 1), 0)
    low = lax.broadcasted_iota(jnp.int32, (1, LANES), 1) < POOL_CH

    def shifted(o, c):
        return ext_ref[pl.ds(hal + o, tm), pl.ds(c * LANES, LANES)]

    def count(w):
        hi = jnp.minimum(t + (w - w // 2), seg)
        lo = jnp.maximum(t - w // 2, 0)
        return (hi - lo).astype(F32)

    def window_sums(c, w_small, w_big):
        acc = shifted(0, c)
        for o in range(-(w_small // 2), w_small - w_small // 2):
            if o != 0:
                acc = acc + shifted(o, c)
        small = acc
        for o in range(-(w_big // 2), w_big - w_big // 2):
            if not (-(w_small // 2) <= o < w_small - w_small // 2):
                acc = acc + shifted(o, c)
        return jnp.where(low, small / count(w_small), acc / count(w_big))

    pooled = jnp.concatenate(
        [window_sums(0, POOL_WINDOWS[0], POOL_WINDOWS[1]),
         window_sums(1, POOL_WINDOWS[2], POOL_WINDOWS[3])], axis=-1) - pin_ref[...]
    p_out = _dot(pooled.astype(BF16), pw_ref[...]) * ps_ref[...]

    y = dsk_ref[...] * u_ref[...] + yf_ref[...] + yb_ref[...]
    g = _gelu_tanh(y)
    glu = g * jax.nn.sigmoid(_dot(g.astype(BF16), gw_ref[...]) + gb_ref[...])
    s_n = _rmsnorm(glu, sn_ref[...])

    cat = jnp.concatenate([a_n, p_out, s_n], axis=-1).astype(BF16)
    o_ref[...] = x_ref[...] + mod_ref[2] * _dot(cat, wo_ref[...])


def _merge(x, mods, attn, pin, u_tm, yf_tm, yb_tm, y_row_off, an, sn, pw, ps, dsk, gw, gb, wo,
           l, row_fn, tm, batch):
    n, d = x.shape
    seg = n // batch
    tpb = seg // tm
    hal = POOL_HALO
    nhb = n // hal
    yo = y_row_off // tm

    def tmaj(off):
        return pl.BlockSpec((tm, S5_W), lambda i: (off + i % tpb, i // tpb))

    def vec(width):
        return pl.BlockSpec((None, 1, width), lambda i: (l, 0, 0))

    def mat(r, c):
        return pl.BlockSpec((None, r, c), lambda i: (l, 0, 0))

    in_specs = [
        pl.BlockSpec((tm, d), lambda i: (i, 0)),
        pl.BlockSpec((None, None, None, 3, 1, d), lambda i: (l, row_fn(i), 1, 0, 0, 0)),
        pl.BlockSpec((tm, ATTN_W), lambda i: (i, 0)),
        pl.BlockSpec((tm, POOL_W), lambda i: (i, 0)),
        pl.BlockSpec((hal, POOL_W), lambda i: (jnp.maximum(i * (tm // hal) - 1, 0), 0)),
        pl.BlockSpec((hal, POOL_W), lambda i: (jnp.minimum((i + 1) * (tm // hal), nhb - 1), 0)),
        tmaj(0), tmaj(yo), tmaj(yo),
        vec(ATTN_W), vec(S5_W), mat(POOL_W, POOL_W), vec(POOL_W), vec(S5_W), mat(S5_W, S5_W), vec(S5_W),
        mat(d, d),
    ]
    return pl.pallas_call(
        functools.partial(_merge_kernel, tm=tm, tpb=tpb, seg=seg),
        grid=(n // tm,),
        in_specs=in_specs,
        out_specs=pl.BlockSpec((tm, d), lambda i: (i, 0)),
        out_shape=jax.ShapeDtypeStruct((n, d), F32),
        scratch_shapes=[pltpu.VMEM((tm + 2 * hal, POOL_W), F32)],
        compiler_params=_params(("arbitrary",)),
        name="merge",
    )(x, mods, attn, pin, pin, pin, u_tm, yf_tm, yb_tm, an, sn, pw, ps, dsk, gw, gb, wo)


def _rope_tables(seq):
    rows = seq // GRID_W
    row = jnp.broadcast_to(jnp.arange(rows, dtype=F32)[:, None], (rows, GRID_W)).reshape(-1)
    col = jnp.broadcast_to(jnp.arange(GRID_W, dtype=F32)[None, :], (rows, GRID_W)).reshape(-1)
    inv = ROPE_THETA ** (-2.0 * jnp.arange(AXIS_FREQS, dtype=F32) / AXIS_DIM)
    ang = jnp.concatenate([row[:, None] * inv, col[:, None] * inv], axis=-1)
    cos, sin = jnp.cos(ang), jnp.sin(ang)
    dim = jnp.arange(LANES) % HEAD_DIM
    idx = (dim // AXIS_DIM) * AXIS_FREQS + dim % AXIS_FREQS
    sign = jnp.where((dim % AXIS_DIM) < AXIS_FREQS, -1.0, 1.0).astype(F32)
    return cos[:, idx], sin[:, idx] * sign


def _block_diag(w):
    dep, g, c, c2 = w.shape
    eye = jnp.eye(g, dtype=w.dtype)
    return jnp.einsum('gk,lgcd->lgckd', eye, w).reshape(dep, g * c, g * c2)


def kernel(x, c, ctx, c_ctx, w_ada, b_ada, norm_sub, w_ffn_gate, w_ffn_up, w_ffn_down, w_in, w_out, q_norm, k_norm, attn_out_norm, ssm_out_norm, pool_w, pool_scale, s5_lam_re, s5_lam_im, s5_log_dt, s5_b_re, s5_b_im, s5_c_re, s5_c_im, s5_d, s5_glu_w, s5_glu_b, final_norm):
    batch, seq, d = x.shape
    ctx_len = ctx.shape[1]
    tm_lat, tm_ctx = 512, 256
    tq = 256
    tc = 128
    lat_tpb = seq // tm_lat

    cond = jnp.concatenate([c_ctx[None], c, jnp.zeros((SUBLANES - 1 - batch, d), F32)], axis=0)
    mods = _adaln(cond, w_ada, b_ada).reshape(DEPTH, SUBLANES, 3, 3, 1, d)
    lat_row = lambda i: 1 + i // lat_tpb
    ctx_row = lambda i: 0

    nsub = norm_sub.reshape(DEPTH, 3, 1, d)
    wg, wu, wd = w_ffn_gate.astype(BF16), w_ffn_up.astype(BF16), w_ffn_down.astype(BF16)
    w_in_b, w_out_b = w_in.astype(BF16), w_out.astype(BF16)
    qn = jnp.tile(q_norm, (1, LANES // HEAD_DIM)).reshape(DEPTH, 1, LANES)
    kn = jnp.tile(k_norm, (1, LANES // HEAD_DIM)).reshape(DEPTH, 1, LANES)
    an = attn_out_norm.reshape(DEPTH, 1, ATTN_W)
    sn = ssm_out_norm.reshape(DEPTH, 1, S5_W)
    pw = _block_diag(pool_w).astype(BF16)
    ps = pool_scale.reshape(DEPTH, 1, POOL_W)
    dsk = s5_d.reshape(DEPTH, 1, S5_W)
    gw = _block_diag(s5_glu_w).astype(BF16)
    gb = s5_glu_b.reshape(DEPTH, 1, S5_W)
    avec, bblk, cblk = _s5_tables(s5_lam_re, s5_lam_im, s5_log_dt, s5_b_re, s5_b_im, s5_c_re, s5_c_im)
    rope_tabs = _rope_tables(seq)

    xs = x.reshape(batch * seq, d)
    zs = ctx.reshape(batch * ctx_len, d)
    for l in range(DEPTH):
        need_ctx_out = l < DEPTH - 1
        xs = _ffn(xs, mods, nsub, wg, wu, wd, l, 0, 0, lat_row, tm_lat)
        zs = _ffn(zs, mods, nsub, wg, wu, wd, l, 0, 0, ctx_row, tm_ctx)

        qx, kx, vx, px, ux = _inproj(xs, mods, nsub, w_in_b, qn, kn, l, lat_row, tm_lat, batch, rope_tabs)
        qz, kz, vz, pz, uz = _inproj(zs, mods, nsub, w_in_b, qn, kn, l, ctx_row, tm_ctx, batch, None)

        def with_ctx(a_ctx, a_lat):
            per = lambda a: a.reshape((batch, -1) + a.shape[1:])
            both = jnp.concatenate([per(a_ctx), per(a_lat)], axis=1)
            return both.reshape((-1,) + both.shape[2:])

        attn_x = _attention(qx, with_ctx(kz, kx), with_ctx(vz, vx), batch, tq)

        u_tb = jnp.concatenate([ux, uz], axis=0).reshape((seq + ctx_len) * batch, S5_W)
        yf, yb = _s5_scan(u_tb, avec, bblk, cblk, l, batch, seq // tc, ctx_len // tc, tc)
        yf = yf.reshape(seq + ctx_len, batch * S5_W)
        yb = yb.reshape(seq + ctx_len, batch * S5_W)

        xs = _merge(xs, mods, attn_x, px, ux, yf, yb, 0, an, sn, pw, ps, dsk, gw, gb, w_out_b,
                    l, lat_row, tm_lat, batch)
        last = l == DEPTH - 1
        xs = _ffn(xs, mods, nsub, wg, wu, wd, l, 2, 1, lat_row, tm_lat,
                  final_g=final_norm if last else None)
        if need_ctx_out:
            attn_z = _attention(qz, kz, vz, batch, tm_ctx)
            zs = _merge(zs, mods, attn_z, pz, uz, yf, yb, seq, an, sn, pw, ps, dsk, gw, gb, w_out_b,
                        l, ctx_row, tm_ctx, batch)
            zs = _ffn(zs, mods, nsub, wg, wu, wd, l, 2, 1, ctx_row, tm_ctx)
    return xs.reshape(batch, seq, d)
```

```python
import functools
import math

import jax
import jax.numpy as jnp
from jax import lax
from jax.experimental import pallas as pl
from jax.experimental.pallas import tpu as pltpu

F32 = jnp.float32
BF16 = jnp.bfloat16

D_MODEL = 1024
DEPTH = 4
GRID_W = 64
N_MOD = 9
EPS = 1e-6
D_FF = 2816
ATTN_W = 512
POOL_W = 256
S5_W = 256
HEAD_DIM = 64
N_Q_HEADS = 8
N_KV_HEADS = 2
Q_PER_KV = 4
KV_W = 128
ROPE_THETA = 10000.0
AXIS_DIM = 32
AXIS_FREQS = 16
POOL_WINDOWS = (2, 4, 8, 16)
POOL_CH = 64
S5_CH = 16
S5_GROUPS = 16
S5_STATE = 64
S5_CHANNELS = S5_GROUPS * S5_STATE
IN_W = ATTN_W + 2 * KV_W + POOL_W + S5_W

LANES = 128
SUBLANES = 8
MXU_DIM = 256
POOL_HALO = 8
KV_CHUNK = MXU_DIM
V_ROWS = HEAD_DIM + 16
FF_SPLITS = (0, 6 * MXU_DIM, D_FF)
VMEM_LIMIT = 58 * 1024 * 1024


def _params(sem):
    return pltpu.CompilerParams(dimension_semantics=sem, vmem_limit_bytes=VMEM_LIMIT)


def _dot(a, b):
    return jnp.dot(a, b, preferred_element_type=F32)


def _silu(x):
    return x * jax.nn.sigmoid(x)


def _rmsnorm(x, g):
    return x * lax.rsqrt(jnp.mean(x * x, axis=-1, keepdims=True) + EPS) * g


def _modulate(x, g, shift, scale):
    return _rmsnorm(x, g) * (1.0 + scale) + shift


def _adaln_kernel(cond_ref, w_ref, b_ref, o_ref):
    s = _silu(cond_ref[...]).astype(BF16)
    o_ref[...] = _dot(s, w_ref[...].astype(BF16)) + b_ref[...]


def _adaln(cond, w_ada, b_ada):
    d = D_MODEL
    tn = 1024
    nj = (N_MOD * d) // tn
    return pl.pallas_call(
        _adaln_kernel,
        grid=(DEPTH, nj),
        in_specs=[
            pl.BlockSpec((SUBLANES, d), lambda l, j: (0, 0)),
            pl.BlockSpec((None, d, tn), lambda l, j: (l, 0, j)),
            pl.BlockSpec((None, 1, tn), lambda l, j: (l, 0, j)),
        ],
        out_specs=pl.BlockSpec((None, SUBLANES, tn), lambda l, j: (l, 0, j)),
        out_shape=jax.ShapeDtypeStruct((DEPTH, SUBLANES, N_MOD * d), F32),
        compiler_params=_params(("arbitrary", "arbitrary")),
        name="adaln",
    )(cond, w_ada, b_ada.reshape(DEPTH, 1, N_MOD * d))


def _zoh_kernel(lr_ref, li_ref, ldt_ref, br_ref, bi_ref, abr_ref, abi_ref, bbr_ref, bbi_ref):
    dt = jnp.exp(ldt_ref[...])
    lr = lr_ref[...]
    li = li_ref[...]
    mag = jnp.exp(lr * dt)
    ab_re = mag * jnp.cos(li * dt)
    ab_im = mag * jnp.sin(li * dt)
    den = lr * lr + li * li
    f_re = ((ab_re - 1.0) * lr + ab_im * li) / den
    f_im = (ab_im * lr - (ab_re - 1.0) * li) / den
    br = br_ref[...]
    bi = bi_ref[...]
    abr_ref[...] = ab_re
    abi_ref[...] = ab_im
    bbr_ref[...] = f_re * br - f_im * bi
    bbi_ref[...] = f_re * bi + f_im * br


def _s5_tables(lam_re, lam_im, log_dt, b_re, b_im, c_re, c_im):
    g, h, p = S5_GROUPS, S5_CH, S5_STATE
    rows = DEPTH * 2 * g * h

    def rep(a):
        return jnp.broadcast_to(a[:, :, :, None, :], (DEPTH, 2, g, h, p)).reshape(rows, p)

    ldt = jnp.broadcast_to(log_dt[:, :, :, None, None], (DEPTH, 2, g, h, p)).reshape(rows, p)
    bt_re = jnp.swapaxes(b_re, -1, -2).reshape(rows, p)
    bt_im = jnp.swapaxes(b_im, -1, -2).reshape(rows, p)
    shp = jax.ShapeDtypeStruct((rows, p), F32)
    ab_re, ab_im, bb_re, bb_im = pl.pallas_call(
        _zoh_kernel, out_shape=(shp, shp, shp, shp), name="s5_zoh",
    )(rep(lam_re), rep(lam_im), ldt, bt_re, bt_im)

    def chan(a):
        return a.reshape(DEPTH, 2, g, h, p)[:, :, :, 0, :].reshape(DEPTH, 2, g * p)

    half = SUBLANES // 2
    avec = jnp.stack([jnp.repeat(chan(ab_re), half, axis=1), jnp.repeat(chan(ab_im), half, axis=1)], axis=1)
    eye = jnp.eye(g, dtype=F32)

    def in_blk(bt):
        bt = bt.reshape(DEPTH, 2, g, h, p)
        return jnp.einsum('gk,ldghp->ldghkp', eye, bt).reshape(DEPTH, 2, g * h, g * p)

    def out_blk(cc):
        return jnp.einsum('gk,ldkhp->ldgpkh', eye, cc).reshape(DEPTH, 2, g * p, g * h)

    bblk = jnp.concatenate([in_blk(bb_re), in_blk(bb_im)], axis=-1).astype(BF16)
    cblk = jnp.concatenate([out_blk(c_re), out_blk(-c_im)], axis=-2).astype(BF16)
    return avec, bblk, cblk


def _mod_spec(l, row_fn, sub):
    return pl.BlockSpec((None, None, None, 3, 1, D_MODEL), lambda i: (l, row_fn(i), sub, 0, 0, 0))


def _ffn_specs(l, sub, which):
    single = pl.Buffered(1)
    d = D_MODEL
    return [
        pl.BlockSpec((None, None, 1, d), lambda i: (l, sub, 0, 0)),
        pl.BlockSpec((None, None, d, D_FF), lambda i: (l, which, 0, 0), pipeline_mode=single),
        pl.BlockSpec((None, None, d, D_FF), lambda i: (l, which, 0, 0), pipeline_mode=single),
        pl.BlockSpec((None, None, D_FF, d), lambda i: (l, which, 0, 0), pipeline_mode=single),
    ]


def _ffn_compute(x, mod_ref, g_ref, wg_ref, wu_ref, wd_ref):
    h = _modulate(x, g_ref[...], mod_ref[0], mod_ref[1]).astype(BF16)
    y = None
    for c0, c1 in zip(FF_SPLITS[:-1], FF_SPLITS[1:]):
        a = _dot(h, wg_ref[:, c0:c1])
        u = _dot(h, wu_ref[:, c0:c1])
        part = _dot((_silu(a) * u).astype(BF16), wd_ref[c0:c1, :])
        y = part if y is None else y + part
    return x + (0.5 * mod_ref[2]) * y


def _ffn_inproj_kernel(x_ref, mod0_ref, g0_ref, wg_ref, wu_ref, wd_ref, mod1_ref, g1_ref, w_ref,
                       qn_ref, kn_ref, *rest, rope):
    if rope:
        cos_ref, sin_ref, y_ref, q_ref, k_ref, v_ref, p_ref, u_ref = rest
    else:
        y_ref, q_ref, k_ref, v_ref, p_ref, u_ref = rest
    x = _ffn_compute(x_ref[...], mod0_ref, g0_ref, wg_ref, wu_ref, wd_ref)
    y_ref[...] = x
    h = _modulate(x, g1_ref[...], mod1_ref[0], mod1_ref[1]).astype(BF16)
    proj = _dot(h, w_ref[...])

    lane = lax.broadcasted_iota(jnp.int32, (1, LANES), 1)
    low = lane < HEAD_DIM
    first_half = (lane & (AXIS_DIM - 1)) < AXIS_FREQS
    er = lax.broadcasted_iota(jnp.int32, (LANES, LANES), 0) // HEAD_DIM
    ec = lax.broadcasted_iota(jnp.int32, (LANES, LANES), 1) // HEAD_DIM
    same_head = (er == ec).astype(BF16)

    def head_norm(blk, w):
        sq = blk * blk
        hi = sq.astype(BF16)
        lo = (sq - hi.astype(F32)).astype(BF16)
        ss = _dot(hi, same_head) + _dot(lo, same_head)
        return blk * lax.rsqrt(ss * (1.0 / HEAD_DIM) + EPS) * w

    def rotary(y):
        if not rope:
            return y
        partner = jnp.where(first_half,
                            pltpu.roll(y, LANES - AXIS_FREQS, 1),
                            pltpu.roll(y, AXIS_FREQS, 1))
        return y * cos_ref[...] + partner * sin_ref[...]

    zero = jnp.zeros((), F32)
    q_scale = HEAD_DIM ** -0.5 * math.log2(math.e)
    for j in range(N_Q_HEADS // 2):
        blk = proj[:, j * LANES:(j + 1) * LANES]
        y = rotary(head_norm(blk, qn_ref[...])) * q_scale
        swapped = pltpu.roll(y, HEAD_DIM, 1)
        if j < N_Q_HEADS // 4:
            even, odd = jnp.where(low, y, zero), jnp.where(low, swapped, zero)
        else:
            even, odd = jnp.where(low, zero, swapped), jnp.where(low, zero, y)
        q_ref[(2 * j) * LANES:(2 * j + 1) * LANES, :] = even.T.astype(BF16)
        q_ref[(2 * j + 1) * LANES:(2 * j + 2) * LANES, :] = odd.T.astype(BF16)

    kblk = proj[:, ATTN_W:ATTN_W + KV_W]
    kk = rotary(head_norm(kblk, kn_ref[...])).astype(BF16)
    vblk = proj[:, ATTN_W + KV_W:ATTN_W + 2 * KV_W]
    ones = jnp.ones((V_ROWS - HEAD_DIM, KV_CHUNK), BF16)
    for c in range(k_ref.shape[0]):
        k_ref[c] = kk[c * KV_CHUNK:(c + 1) * KV_CHUNK, :]
        vt = vblk[c * KV_CHUNK:(c + 1) * KV_CHUNK, :].T.astype(BF16)
        for g in range(N_KV_HEADS):
            v_ref[c, g, 0:HEAD_DIM, :] = vt[g * HEAD_DIM:(g + 1) * HEAD_DIM, :]
            v_ref[c, g, HEAD_DIM:V_ROWS, :] = ones

    p_ref[...] = proj[:, ATTN_W + 2 * KV_W:ATTN_W + 2 * KV_W + POOL_W]
    u_ref[...] = proj[:, ATTN_W + 2 * KV_W + POOL_W:]


def _ffn_inproj(x, mods, norm_sub, wg, wu, wd, w_in, qn, kn, l, row_fn, tm, batch, rope_tabs):
    n, d = x.shape
    seg = n // batch
    tpb = seg // tm
    rope = rope_tabs is not None
    in_specs = (
        [pl.BlockSpec((tm, d), lambda i: (i, 0)), _mod_spec(l, row_fn, 0)]
        + _ffn_specs(l, 0, 0)
        + [_mod_spec(l, row_fn, 1),
           pl.BlockSpec((None, None, 1, d), lambda i: (l, 1, 0, 0)),
           pl.BlockSpec((None, d, IN_W), lambda i: (l, 0, 0), pipeline_mode=pl.Buffered(1)),
           pl.BlockSpec((None, 1, LANES), lambda i: (l, 0, 0)),
           pl.BlockSpec((None, 1, LANES), lambda i: (l, 0, 0))])
    args = [x, mods, norm_sub, wg, wu, wd, mods, norm_sub, w_in, qn, kn]
    if rope:
        in_specs += [pl.BlockSpec((tm, LANES), lambda i: (i % tpb, 0))] * 2
        args += list(rope_tabs)
    cpt = tm // KV_CHUNK
    out_shape = (
        jax.ShapeDtypeStruct((n, d), F32),
        jax.ShapeDtypeStruct((N_Q_HEADS * LANES, n), BF16),
        jax.ShapeDtypeStruct((n // KV_CHUNK, KV_CHUNK, KV_W), BF16),
        jax.ShapeDtypeStruct((n // KV_CHUNK, N_KV_HEADS, V_ROWS, KV_CHUNK), BF16),
        jax.ShapeDtypeStruct((n, POOL_W), F32),
        jax.ShapeDtypeStruct((seg, batch * S5_W), F32),
    )
    out_specs = (
        pl.BlockSpec((tm, d), lambda i: (i, 0)),
        pl.BlockSpec((N_Q_HEADS * LANES, tm), lambda i: (0, i)),
        pl.BlockSpec((cpt, KV_CHUNK, KV_W), lambda i: (i, 0, 0)),
        pl.BlockSpec((cpt, N_KV_HEADS, V_ROWS, KV_CHUNK), lambda i: (i, 0, 0, 0)),
        pl.BlockSpec((tm, POOL_W), lambda i: (i, 0)),
        pl.BlockSpec((tm, S5_W), lambda i: (i % tpb, i // tpb)),
    )
    return pl.pallas_call(
        functools.partial(_ffn_inproj_kernel, rope=rope),
        grid=(n // tm,),
        in_specs=in_specs,
        out_specs=out_specs,
        out_shape=out_shape,
        compiler_params=_params(("arbitrary",)),
        name="ffn_inproj",
    )(*args)


def _attn_kernel(*refs, n_lat):
    if n_lat:
        q_ref, kc_ref, vc_ref, kl_ref, vl_ref, o_ref, m_ref, acc_ref, ot_ref, sa_ref, sb_ref = refs
    else:
        q_ref, kc_ref, vc_ref, o_ref, m_ref, acc_ref, ot_ref, sa_ref, sb_ref = refs
    m_ref[...] = jnp.full(m_ref.shape, -1e30, F32)
    acc_ref[...] = jnp.zeros(acc_ref.shape, F32)

    def step(v_of_head, cur_ref, k_next, nxt_ref):
        for h in range(N_Q_HEADS):
            if k_next is not None:
                nxt_ref[h] = _dot(k_next, q_ref[h * LANES:(h + 1) * LANES, :])
            m_prev = m_ref[h]
            m_new = jnp.maximum(m_prev, jnp.max(cur_ref[h], axis=0, keepdims=True))
            alpha = jnp.exp2(m_prev - m_new)
            m_ref[h] = m_new
            pt = jnp.exp2(cur_ref[h] - m_new).astype(BF16)
            acc_ref[h] = alpha * acc_ref[h] + _dot(v_of_head(h // Q_PER_KV), pt)

    kc = kc_ref[0]
    for h in range(N_Q_HEADS):
        sa_ref[h] = _dot(kc, q_ref[h * LANES:(h + 1) * LANES, :])
    ctx_v = lambda g: vc_ref[0, g]
    if not n_lat:
        step(ctx_v, sa_ref, None, None)
    else:
        lat_v = lambda j: (lambda g: vl_ref[j, g])
        step(ctx_v, sa_ref, kl_ref[0], sb_ref)

        def body(i, carry):
            step(lat_v(2 * i), sb_ref, kl_ref[2 * i + 1], sa_ref)
            step(lat_v(2 * i + 1), sa_ref, kl_ref[2 * i + 2], sb_ref)
            return carry

        pairs = (n_lat - 1) // 2
        lax.fori_loop(0, pairs, body, 0)
        if n_lat - 2 * pairs == 2:
            step(lat_v(n_lat - 2), sb_ref, kl_ref[n_lat - 1], sa_ref)
            step(lat_v(n_lat - 1), sa_ref, None, None)
        else:
            step(lat_v(n_lat - 1), sb_ref, None, None)
    for h in range(N_Q_HEADS):
        ot_ref[h * HEAD_DIM:(h + 1) * HEAD_DIM, :] = (
            acc_ref[h, 0:HEAD_DIM, :] / acc_ref[h, HEAD_DIM:HEAD_DIM + 1, :])
    o_ref[...] = ot_ref[...].T


def _attention(qt, k_ctx, vt_ctx, k_lat, vt_lat, batch, tq):
    n = qt.shape[1]
    nq = n // batch // tq
    n_lat = 0 if k_lat is None else k_lat.shape[0] // batch
    in_specs = [
        pl.BlockSpec((N_Q_HEADS * LANES, tq), lambda b, i: (0, b * nq + i)),
        pl.BlockSpec((1, KV_CHUNK, KV_W), lambda b, i: (b, 0, 0)),
        pl.BlockSpec((1, N_KV_HEADS, V_ROWS, KV_CHUNK), lambda b, i: (b, 0, 0, 0)),
    ]
    args = [qt, k_ctx, vt_ctx]
    if n_lat:
        in_specs += [
            pl.BlockSpec((n_lat, KV_CHUNK, KV_W), lambda b, i: (b, 0, 0)),
            pl.BlockSpec((n_lat, N_KV_HEADS, V_ROWS, KV_CHUNK), lambda b, i: (b, 0, 0, 0)),
        ]
        args += [k_lat, vt_lat]
    return pl.pallas_call(
        functools.partial(_attn_kernel, n_lat=n_lat),
        grid=(batch, nq),
        in_specs=in_specs,
        out_specs=pl.BlockSpec((tq, ATTN_W), lambda b, i: (b * nq + i, 0)),
        out_shape=jax.ShapeDtypeStruct((n, ATTN_W), F32),
        scratch_shapes=[
            pltpu.VMEM((N_Q_HEADS, 1, tq), F32),
            pltpu.VMEM((N_Q_HEADS, V_ROWS, tq), F32),
            pltpu.VMEM((ATTN_W, tq), F32),
            pltpu.VMEM((N_Q_HEADS, KV_CHUNK, tq), F32),
            pltpu.VMEM((N_Q_HEADS, KV_CHUNK, tq), F32),
        ],
        compiler_params=_params(("arbitrary", "arbitrary")),
        name="attention",
    )(*args)


def _s5_kernel(ufl_ref, ufc_ref, ubl_ref, ubc_ref, a_ref, bblk_ref, cblk_ref, yf_ref, yb_ref,
               sf_ref, sb_ref, st_ref, *, tc, batch, n_ctx_tiles, n_tiles):
    nch = S5_CHANNELS
    i = pl.program_id(0)

    @pl.when(i == 0)
    def _():
        st_ref[...] = jnp.zeros(st_ref.shape, F32)

    assert 2 * batch == SUBLANES
    ngrp = tc * batch // SUBLANES
    uf = jnp.where(i < n_ctx_tiles, ufc_ref[...], ufl_ref[...])
    ub = jnp.where(i < n_ctx_tiles, ubc_ref[...], ubl_ref[...])
    sf_ref[...] = _dot(uf.astype(BF16), bblk_ref[0]).reshape(ngrp, SUBLANES, 2 * nch)
    sb_ref[...] = _dot(ub.astype(BF16), bblk_ref[1]).reshape(ngrp, SUBLANES, 2 * nch)
    a_re, a_im = a_ref[0], a_ref[1]
    top = lax.broadcasted_iota(jnp.int32, (SUBLANES, 1), 0) < batch

    def advance(s_re, s_im, x):
        return (a_re * s_re - a_im * s_im + x[:, :nch], a_re * s_im + a_im * s_re + x[:, nch:])

    def body(g, carry):
        s_re, s_im = carry
        gb = ngrp - 1 - g
        f = sf_ref[g]
        b = sb_ref[gb]
        first = jnp.where(top, f, b)
        second = pltpu.roll(jnp.where(top, b, f), batch, 0)
        r1, i1 = advance(s_re, s_im, first)
        r2, i2 = advance(r1, i1, second)
        s1 = jnp.concatenate([r1, i1], axis=1)
        s2 = pltpu.roll(jnp.concatenate([r2, i2], axis=1), batch, 0)
        sf_ref[g] = jnp.where(top, s1, s2)
        sb_ref[gb] = jnp.where(top, s2, s1)
        return r2, i2

    s_re, s_im = lax.fori_loop(0, ngrp, body, (st_ref[0], st_ref[1]))
    st_ref[0] = s_re
    st_ref[1] = s_im
    rows = ngrp * SUBLANES
    yf_ref[...] = _dot(sf_ref[...].reshape(rows, 2 * nch).astype(BF16), cblk_ref[0])
    yb_ref[...] = _dot(sb_ref[...].reshape(rows, 2 * nch).astype(BF16), cblk_ref[1])


def _s5_scan(u_lat, u_ctx, avec, bblk, cblk, l, batch, tc):
    rows = tc * batch
    n_lat_tiles = u_lat.shape[0] // rows
    n_ctx_tiles = u_ctx.shape[0] // rows
    n_tiles = n_lat_tiles + n_ctx_tiles
    nch = S5_CHANNELS

    def fwd(i):
        return jnp.where(i < n_ctx_tiles, n_lat_tiles + i, i - n_ctx_tiles)

    def bwd(i):
        return n_tiles - 1 - i

    def lat_of(tile):
        return lambda i: (jnp.clip(tile(i), 0, n_lat_tiles - 1), 0)

    def ctx_of(tile):
        return lambda i: (jnp.clip(tile(i) - n_lat_tiles, 0, n_ctx_tiles - 1), 0)

    blk = (rows, S5_W)
    shp = jax.ShapeDtypeStruct((n_tiles * rows, S5_W), F32)
    return pl.pallas_call(
        functools.partial(_s5_kernel, tc=tc, batch=batch, n_ctx_tiles=n_ctx_tiles, n_tiles=n_tiles),
        grid=(n_tiles,),
        in_specs=[
            pl.BlockSpec(blk, lat_of(fwd)), pl.BlockSpec(blk, ctx_of(fwd)),
            pl.BlockSpec(blk, lat_of(bwd)), pl.BlockSpec(blk, ctx_of(bwd)),
            pl.BlockSpec((None, 2, SUBLANES, nch), lambda i: (l, 0, 0, 0)),
            pl.BlockSpec((None, 2, S5_W, 2 * nch), lambda i: (l, 0, 0, 0)),
            pl.BlockSpec((None, 2, 2 * nch, S5_W), lambda i: (l, 0, 0, 0)),
        ],
        out_specs=(
            pl.BlockSpec(blk, lambda i: (fwd(i), 0)),
            pl.BlockSpec(blk, lambda i: (bwd(i), 0)),
        ),
        out_shape=(shp, shp),
        scratch_shapes=[
            pltpu.VMEM((rows // SUBLANES, SUBLANES, 2 * nch), F32),
            pltpu.VMEM((rows // SUBLANES, SUBLANES, 2 * nch), F32),
            pltpu.VMEM((2, SUBLANES, nch), F32),
        ],
        compiler_params=_params(("arbitrary",)),
        name="s5_scan",
    )(u_lat, u_ctx, u_lat, u_ctx, avec, bblk, cblk)


def _gelu_tanh(x):
    return 0.5 * x * (1.0 + jnp.tanh(math.sqrt(2.0 / math.pi) * (x + 0.044715 * (x * x * x))))


def _merge_ffn_kernel(x_ref, mod1_ref, attn_ref, pin_ref, prev_ref, next_ref, u_ref, yf_ref, yb_ref,
                      an_ref, sn_ref, pw_ref, ps_ref, dsk_ref, gw_ref, gb_ref, wo_ref,
                      mod2_ref, g2_ref, wg_ref, wu_ref, wd_ref, *rest, tm, tpb, seg, final):
    if final:
        fg_ref, o_ref, ext_ref = rest
    else:
        o_ref, ext_ref = rest
    ti = pl.program_id(0) % tpb
    a_n = _rmsnorm(attn_ref[...], an_ref[...])

    hal = POOL_HALO
    ext_ref[pl.ds(0, hal), :] = jnp.where(ti > 0, prev_ref[...], 0.0)
    ext_ref[pl.ds(hal, tm), :] = pin_ref[...]
    ext_ref[pl.ds(hal + tm, hal), :] = jnp.where(ti < tpb - 1, next_ref[...], 0.0)
    t = ti * tm + lax.broadcasted_iota(jnp.int32, (tm, 1), 0)
    low = lax.broadcasted_iota(jnp.int32, (1, LANES), 1) < POOL_CH

    def shifted(o, c):
        return ext_ref[pl.ds(hal + o, tm), pl.ds(c * LANES, LANES)]

    def count(w):
        hi = jnp.minimum(t + (w - w // 2), seg)
        lo = jnp.maximum(t - w // 2, 0)
        return (hi - lo).astype(F32)

    def window_sums(c, w_small, w_big):
        acc = shifted(0, c)
        for o in range(-(w_small // 2), w_small - w_small // 2):
            if o != 0:
                acc = acc + shifted(o, c)
        small = acc
        for o in range(-(w_big // 2), w_big - w_big // 2):
            if not (-(w_small // 2) <= o < w_small - w_small // 2):
                acc = acc + shifted(o, c)
        return jnp.where(low, small / count(w_small), acc / count(w_big))

    pooled = jnp.concatenate(
        [window_sums(0, POOL_WINDOWS[0], POOL_WINDOWS[1]),
         window_sums(1, POOL_WINDOWS[2], POOL_WINDOWS[3])], axis=-1) - pin_ref[...]
    p_out = _dot(pooled.astype(BF16), pw_ref[...]) * ps_ref[...]

    y = dsk_ref[...] * u_ref[...] + yf_ref[...] + yb_ref[...]
    g = _gelu_tanh(y)
    glu = g * jax.nn.sigmoid(_dot(g.astype(BF16), gw_ref[...]) + gb_ref[...])
    s_n = _rmsnorm(glu, sn_ref[...])

    cat = jnp.concatenate([a_n, p_out, s_n], axis=-1).astype(BF16)
    x = x_ref[...] + mod1_ref[2] * _dot(cat, wo_ref[...])
    x = _ffn_compute(x, mod2_ref, g2_ref, wg_ref, wu_ref, wd_ref)
    if final:
        x = _rmsnorm(x, fg_ref[...])
    o_ref[...] = x


def _merge_ffn(x, mods, norm_sub, attn, pin, u_tm, yf_tm, yb_tm, y_row_off, an, sn, pw, ps, dsk, gw, gb, wo,
               wg, wu, wd, l, row_fn, tm, batch, final_g=None):
    n, d = x.shape
    seg = n // batch
    tpb = seg // tm
    hal = POOL_HALO
    nhb = n // hal
    yo = y_row_off // tm
    single = pl.Buffered(1)

    def tmaj(off):
        return pl.BlockSpec((tm, S5_W), lambda i: (off + i % tpb, i // tpb))

    def vec(width):
        return pl.BlockSpec((None, 1, width), lambda i: (l, 0, 0))

    def mat(r, c):
        return pl.BlockSpec((None, r, c), lambda i: (l, 0, 0), pipeline_mode=single)

    in_specs = [
        pl.BlockSpec((tm, d), lambda i: (i, 0)),
        _mod_spec(l, row_fn, 1),
        pl.BlockSpec((tm, ATTN_W), lambda i: (i, 0)),
        pl.BlockSpec((tm, POOL_W), lambda i: (i, 0)),
        pl.BlockSpec((hal, POOL_W), lambda i: (jnp.maximum(i * (tm // hal) - 1, 0), 0)),
        pl.BlockSpec((hal, POOL_W), lambda i: (jnp.minimum((i + 1) * (tm // hal), nhb - 1), 0)),
        tmaj(0), tmaj(yo), tmaj(yo),
        vec(ATTN_W), vec(S5_W), mat(POOL_W, POOL_W), vec(POOL_W), vec(S5_W), mat(S5_W, S5_W), vec(S5_W),
        mat(d, d),
        _mod_spec(l, row_fn, 2),
    ] + _ffn_specs(l, 2, 1)
    args = [x, mods, attn, pin, pin, pin, u_tm, yf_tm, yb_tm, an, sn, pw, ps, dsk, gw, gb, wo,
            mods, norm_sub, wg, wu, wd]
    if final_g is not None:
        in_specs.append(pl.BlockSpec((1, d), lambda i: (0, 0)))
        args.append(final_g.reshape(1, d))
    return pl.pallas_call(
        functools.partial(_merge_ffn_kernel, tm=tm, tpb=tpb, seg=seg, final=final_g is not None),
        grid=(n // tm,),
        in_specs=in_specs,
        out_specs=pl.BlockSpec((tm, d), lambda i: (i, 0)),
        out_shape=jax.ShapeDtypeStruct((n, d), F32),
        scratch_shapes=[pltpu.VMEM((tm + 2 * hal, POOL_W), F32)],
        compiler_params=_params(("arbitrary",)),
        name="merge_ffn",
    )(*args)


def _rope_tables(seq):
    rows = seq // GRID_W
    row = jnp.broadcast_to(jnp.arange(rows, dtype=F32)[:, None], (rows, GRID_W)).reshape(-1)
    col = jnp.broadcast_to(jnp.arange(GRID_W, dtype=F32)[None, :], (rows, GRID_W)).reshape(-1)
    inv = ROPE_THETA ** (-2.0 * jnp.arange(AXIS_FREQS, dtype=F32) / AXIS_DIM)
    ang = jnp.concatenate([row[:, None] * inv, col[:, None] * inv], axis=-1)
    cos, sin = jnp.cos(ang), jnp.sin(ang)
    dim = jnp.arange(LANES) % HEAD_DIM
    idx = (dim // AXIS_DIM) * AXIS_FREQS + dim % AXIS_FREQS
    sign = jnp.where((dim % AXIS_DIM) < AXIS_FREQS, -1.0, 1.0).astype(F32)
    return cos[:, idx], sin[:, idx] * sign


def _block_diag(w):
    dep, g, c, c2 = w.shape
    eye = jnp.eye(g, dtype=w.dtype)
    return jnp.einsum('gk,lgcd->lgckd', eye, w).reshape(dep, g * c, g * c2)


def kernel(x, c, ctx, c_ctx, w_ada, b_ada, norm_sub, w_ffn_gate, w_ffn_up, w_ffn_down, w_in, w_out, q_norm, k_norm, attn_out_norm, ssm_out_norm, pool_w, pool_scale, s5_lam_re, s5_lam_im, s5_log_dt, s5_b_re, s5_b_im, s5_c_re, s5_c_im, s5_d, s5_glu_w, s5_glu_b, final_norm):
    batch, seq, d = x.shape
    ctx_len = ctx.shape[1]
    tm_lat, tm_ctx = 512, 256
    tq = 512
    tc = 128
    lat_tpb = seq // tm_lat

    cond = jnp.concatenate([c_ctx[None], c, jnp.zeros((SUBLANES - 1 - batch, d), F32)], axis=0)
    mods = _adaln(cond, w_ada, b_ada).reshape(DEPTH, SUBLANES, 3, 3, 1, d)
    lat_row = lambda i: 1 + i // lat_tpb
    ctx_row = lambda i: 0

    nsub = norm_sub.reshape(DEPTH, 3, 1, d)
    wg, wu, wd = w_ffn_gate.astype(BF16), w_ffn_up.astype(BF16), w_ffn_down.astype(BF16)
    w_in_b, w_out_b = w_in.astype(BF16), w_out.astype(BF16)
    qn = jnp.tile(q_norm, (1, LANES // HEAD_DIM)).reshape(DEPTH, 1, LANES)
    kn = jnp.tile(k_norm, (1, LANES // HEAD_DIM)).reshape(DEPTH, 1, LANES)
    an = attn_out_norm.reshape(DEPTH, 1, ATTN_W)
    sn = ssm_out_norm.reshape(DEPTH, 1, S5_W)
    pw = _block_diag(pool_w).astype(BF16)
    ps = pool_scale.reshape(DEPTH, 1, POOL_W)
    dsk = s5_d.reshape(DEPTH, 1, S5_W)
    gw = _block_diag(s5_glu_w).astype(BF16)
    gb = s5_glu_b.reshape(DEPTH, 1, S5_W)
    avec, bblk, cblk = _s5_tables(s5_lam_re, s5_lam_im, s5_log_dt, s5_b_re, s5_b_im, s5_c_re, s5_c_im)
    rope_tabs = _rope_tables(seq)

    xs = x.reshape(batch * seq, d)
    zs = ctx.reshape(batch * ctx_len, d)
    for l in range(DEPTH):
        last = l == DEPTH - 1
        xs, qx, kx, vx, px, ux = _ffn_inproj(xs, mods, nsub, wg, wu, wd, w_in_b, qn, kn, l, lat_row,
                                             tm_lat, batch, rope_tabs)
        zs, qz, kz, vz, pz, uz = _ffn_inproj(zs, mods, nsub, wg, wu, wd, w_in_b, qn, kn, l, ctx_row,
                                             tm_ctx, batch, None)

        attn_x = _attention(qx, kz, vz, kx, vx, batch, tq)

        yf, yb = _s5_scan(ux.reshape(seq * batch, S5_W), uz.reshape(ctx_len * batch, S5_W),
                          avec, bblk, cblk, l, batch, tc)
        yf = yf.reshape(seq + ctx_len, batch * S5_W)
        yb = yb.reshape(seq + ctx_len, batch * S5_W)

        xs = _merge_ffn(xs, mods, nsub, attn_x, px, ux, yf, yb, 0, an, sn, pw, ps, dsk, gw, gb, w_out_b,
                        wg, wu, wd, l, lat_row, tm_lat, batch, final_g=final_norm if last else None)
        if not last:
            attn_z = _attention(qz, kz, vz, None, None, batch, tm_ctx)
            zs = _merge_ffn(zs, mods, nsub, attn_z, pz, uz, yf, yb, seq, an, sn, pw, ps, dsk, gw, gb,
                            w_out_b, wg, wu, wd, l, ctx_row, tm_ctx, batch)
    return xs.reshape(batch, seq, d)
```

```python
import functools
import math

import jax
import jax.numpy as jnp
from jax import lax
from jax.experimental import pallas as pl
from jax.experimental.pallas import tpu as pltpu

F32 = jnp.float32
BF16 = jnp.bfloat16

D_MODEL = 1024
DEPTH = 4
GRID_W = 64
N_MOD = 9
EPS = 1e-6
D_FF = 2816
ATTN_W = 512
POOL_W = 256
S5_W = 256
HEAD_DIM = 64
N_Q_HEADS = 8
N_KV_HEADS = 2
Q_PER_KV = 4
KV_W = 128
ROPE_THETA = 10000.0
AXIS_DIM = 32
AXIS_FREQS = 16
POOL_WINDOWS = (2, 4, 8, 16)
POOL_CH = 64
S5_CH = 16
S5_GROUPS = 16
S5_STATE = 64
S5_CHANNELS = S5_GROUPS * S5_STATE
IN_W = ATTN_W + 2 * KV_W + POOL_W + S5_W

LANES = 128
SUBLANES = 8
MXU_DIM = 256
POOL_HALO = 8
KV_CHUNK = MXU_DIM
V_ROWS = HEAD_DIM + 16
FF_SPLITS = (0, 6 * MXU_DIM, D_FF)
VMEM_LIMIT = 58 * 1024 * 1024


def _params(sem):
    return pltpu.CompilerParams(dimension_semantics=sem, vmem_limit_bytes=VMEM_LIMIT)


def _dot(a, b):
    return jnp.dot(a, b, preferred_element_type=F32)


def _silu(x):
    return x * jax.nn.sigmoid(x)


def _rmsnorm(x, g):
    return x * lax.rsqrt(jnp.mean(x * x, axis=-1, keepdims=True) + EPS) * g


def _modulate(x, g, shift, scale):
    return _rmsnorm(x, g) * (1.0 + scale) + shift


def _adaln_kernel(cond_ref, w_ref, b_ref, o_ref):
    s = _silu(cond_ref[...]).astype(BF16)
    o_ref[...] = _dot(s, w_ref[...].astype(BF16)) + b_ref[...]


def _adaln(cond, w_ada, b_ada):
    d = D_MODEL
    tn = 1024
    nj = (N_MOD * d) // tn
    return pl.pallas_call(
        _adaln_kernel,
        grid=(DEPTH, nj),
        in_specs=[
            pl.BlockSpec((SUBLANES, d), lambda l, j: (0, 0)),
            pl.BlockSpec((None, d, tn), lambda l, j: (l, 0, j)),
            pl.BlockSpec((None, 1, tn), lambda l, j: (l, 0, j)),
        ],
        out_specs=pl.BlockSpec((None, SUBLANES, tn), lambda l, j: (l, 0, j)),
        out_shape=jax.ShapeDtypeStruct((DEPTH, SUBLANES, N_MOD * d), F32),
        compiler_params=_params(("arbitrary", "arbitrary")),
        name="adaln",
    )(cond, w_ada, b_ada.reshape(DEPTH, 1, N_MOD * d))


def _zoh_kernel(lr_ref, li_ref, ldt_ref, br_ref, bi_ref, abr_ref, abi_ref, bbr_ref, bbi_ref):
    dt = jnp.exp(ldt_ref[...])
    lr = lr_ref[...]
    li = li_ref[...]
    mag = jnp.exp(lr * dt)
    ab_re = mag * jnp.cos(li * dt)
    ab_im = mag * jnp.sin(li * dt)
    den = lr * lr + li * li
    f_re = ((ab_re - 1.0) * lr + ab_im * li) / den
    f_im = (ab_im * lr - (ab_re - 1.0) * li) / den
    br = br_ref[...]
    bi = bi_ref[...]
    abr_ref[...] = ab_re
    abi_ref[...] = ab_im
    bbr_ref[...] = f_re * br - f_im * bi
    bbi_ref[...] = f_re * bi + f_im * br


def _s5_tables(lam_re, lam_im, log_dt, b_re, b_im, c_re, c_im):
    g, h, p = S5_GROUPS, S5_CH, S5_STATE
    rows = DEPTH * 2 * g * h

    def rep(a):
        return jnp.broadcast_to(a[:, :, :, None, :], (DEPTH, 2, g, h, p)).reshape(rows, p)

    ldt = jnp.broadcast_to(log_dt[:, :, :, None, None], (DEPTH, 2, g, h, p)).reshape(rows, p)
    bt_re = jnp.swapaxes(b_re, -1, -2).reshape(rows, p)
    bt_im = jnp.swapaxes(b_im, -1, -2).reshape(rows, p)
    shp = jax.ShapeDtypeStruct((rows, p), F32)
    ab_re, ab_im, bb_re, bb_im = pl.pallas_call(
        _zoh_kernel, out_shape=(shp, shp, shp, shp), name="s5_zoh",
    )(rep(lam_re), rep(lam_im), ldt, bt_re, bt_im)

    def chan(a):
        return a.reshape(DEPTH, 2, g, h, p)[:, :, :, 0, :].reshape(DEPTH, 2, g * p)

    half = SUBLANES // 2
    avec = jnp.stack([jnp.repeat(chan(ab_re), half, axis=1), jnp.repeat(chan(ab_im), half, axis=1)], axis=1)
    eye = jnp.eye(g, dtype=F32)

    def in_blk(bt):
        bt = bt.reshape(DEPTH, 2, g, h, p)
        return jnp.einsum('gk,ldghp->ldghkp', eye, bt).reshape(DEPTH, 2, g * h, g * p)

    def out_blk(cc):
        return jnp.einsum('gk,ldkhp->ldgpkh', eye, cc).reshape(DEPTH, 2, g * p, g * h)

    bblk = jnp.concatenate([in_blk(bb_re), in_blk(bb_im)], axis=-1).astype(BF16)
    cblk = jnp.concatenate([out_blk(c_re), out_blk(-c_im)], axis=-2).astype(BF16)
    return avec, bblk, cblk


def _mod_spec(l, row_fn, sub):
    return pl.BlockSpec((None, None, None, 3, 1, D_MODEL), lambda i: (l, row_fn(i), sub, 0, 0, 0))


def _ffn_specs(l, sub, which):
    single = pl.Buffered(1)
    d = D_MODEL
    return [
        pl.BlockSpec((None, None, 1, d), lambda i: (l, sub, 0, 0)),
        pl.BlockSpec((None, None, d, D_FF), lambda i: (l, which, 0, 0), pipeline_mode=single),
        pl.BlockSpec((None, None, d, D_FF), lambda i: (l, which, 0, 0), pipeline_mode=single),
        pl.BlockSpec((None, None, D_FF, d), lambda i: (l, which, 0, 0), pipeline_mode=single),
    ]


def _ffn_compute(x, mod_ref, g_ref, wg_ref, wu_ref, wd_ref):
    half = x.shape[0] // 2
    xs = (x[:half], x[half:])
    hs = [_modulate(xh, g_ref[...], mod_ref[0], mod_ref[1]).astype(BF16) for xh in xs]
    ys = [None, None]
    for c0, c1 in zip(FF_SPLITS[:-1], FF_SPLITS[1:]):
        for r in range(2):
            a = _dot(hs[r], wg_ref[:, c0:c1])
            u = _dot(hs[r], wu_ref[:, c0:c1])
            part = _dot((_silu(a) * u).astype(BF16), wd_ref[c0:c1, :])
            ys[r] = part if ys[r] is None else ys[r] + part
    return jnp.concatenate([xs[r] + (0.5 * mod_ref[2]) * ys[r] for r in range(2)], axis=0)


def _ffn_inproj_kernel(x_ref, mod0_ref, g0_ref, wg_ref, wu_ref, wd_ref, mod1_ref, g1_ref, w_ref,
                       qn_ref, kn_ref, *rest, rope):
    if rope:
        cos_ref, sin_ref, y_ref, q_ref, k_ref, v_ref, p_ref, u_ref = rest
    else:
        y_ref, q_ref, k_ref, v_ref, p_ref, u_ref = rest
    x = _ffn_compute(x_ref[...], mod0_ref, g0_ref, wg_ref, wu_ref, wd_ref)
    y_ref[...] = x
    h = _modulate(x, g1_ref[...], mod1_ref[0], mod1_ref[1]).astype(BF16)
    proj = _dot(h, w_ref[...])

    lane = lax.broadcasted_iota(jnp.int32, (1, LANES), 1)
    low = lane < HEAD_DIM
    first_half = (lane & (AXIS_DIM - 1)) < AXIS_FREQS
    er = lax.broadcasted_iota(jnp.int32, (LANES, LANES), 0) // HEAD_DIM
    ec = lax.broadcasted_iota(jnp.int32, (LANES, LANES), 1) // HEAD_DIM
    same_head = (er == ec).astype(BF16)

    def head_norm(blk, w):
        sq = blk * blk
        hi = sq.astype(BF16)
        lo = (sq - hi.astype(F32)).astype(BF16)
        ss = _dot(hi, same_head) + _dot(lo, same_head)
        return blk * lax.rsqrt(ss * (1.0 / HEAD_DIM) + EPS) * w

    def rotary(y):
        if not rope:
            return y
        partner = jnp.where(first_half,
                            pltpu.roll(y, LANES - AXIS_FREQS, 1),
                            pltpu.roll(y, AXIS_FREQS, 1))
        return y * cos_ref[...] + partner * sin_ref[...]

    zero = jnp.zeros((), F32)
    q_scale = HEAD_DIM ** -0.5 * math.log2(math.e)
    for j in range(N_Q_HEADS // 2):
        blk = proj[:, j * LANES:(j + 1) * LANES]
        y = rotary(head_norm(blk, qn_ref[...])) * q_scale
        swapped = pltpu.roll(y, HEAD_DIM, 1)
        if j < N_Q_HEADS // 4:
            even, odd = jnp.where(low, y, zero), jnp.where(low, swapped, zero)
        else:
            even, odd = jnp.where(low, zero, swapped), jnp.where(low, zero, y)
        q_ref[(2 * j) * LANES:(2 * j + 1) * LANES, :] = even.T.astype(BF16)
        q_ref[(2 * j + 1) * LANES:(2 * j + 2) * LANES, :] = odd.T.astype(BF16)

    kblk = proj[:, ATTN_W:ATTN_W + KV_W]
    kk = rotary(head_norm(kblk, kn_ref[...])).astype(BF16)
    vblk = proj[:, ATTN_W + KV_W:ATTN_W + 2 * KV_W]
    ones = jnp.ones((V_ROWS - HEAD_DIM, KV_CHUNK), BF16)
    for c in range(k_ref.shape[0]):
        k_ref[c] = kk[c * KV_CHUNK:(c + 1) * KV_CHUNK, :]
        vt = vblk[c * KV_CHUNK:(c + 1) * KV_CHUNK, :].T.astype(BF16)
        for g in range(N_KV_HEADS):
            v_ref[c, g, 0:HEAD_DIM, :] = vt[g * HEAD_DIM:(g + 1) * HEAD_DIM, :]
            v_ref[c, g, HEAD_DIM:V_ROWS, :] = ones

    p_ref[...] = proj[:, ATTN_W + 2 * KV_W:ATTN_W + 2 * KV_W + POOL_W]
    u_ref[...] = proj[:, ATTN_W + 2 * KV_W + POOL_W:]


def _ffn_inproj(x, mods, norm_sub, wg, wu, wd, w_in, qn, kn, l, row_fn, tm, batch, rope_tabs):
    n, d = x.shape
    seg = n // batch
    tpb = seg // tm
    rope = rope_tabs is not None
    in_specs = (
        [pl.BlockSpec((tm, d), lambda i: (i, 0)), _mod_spec(l, row_fn, 0)]
        + _ffn_specs(l, 0, 0)
        + [_mod_spec(l, row_fn, 1),
           pl.BlockSpec((None, None, 1, d), lambda i: (l, 1, 0, 0)),
           pl.BlockSpec((None, d, IN_W), lambda i: (l, 0, 0), pipeline_mode=pl.Buffered(1)),
           pl.BlockSpec((None, 1, LANES), lambda i: (l, 0, 0)),
           pl.BlockSpec((None, 1, LANES), lambda i: (l, 0, 0))])
    args = [x, mods, norm_sub, wg, wu, wd, mods, norm_sub, w_in, qn, kn]
    if rope:
        in_specs += [pl.BlockSpec((tm, LANES), lambda i: (i % tpb, 0))] * 2
        args += list(rope_tabs)
    cpt = tm // KV_CHUNK
    out_shape = (
        jax.ShapeDtypeStruct((n, d), F32),
        jax.ShapeDtypeStruct((N_Q_HEADS * LANES, n), BF16),
        jax.ShapeDtypeStruct((n // KV_CHUNK, KV_CHUNK, KV_W), BF16),
        jax.ShapeDtypeStruct((n // KV_CHUNK, N_KV_HEADS, V_ROWS, KV_CHUNK), BF16),
        jax.ShapeDtypeStruct((n, POOL_W), F32),
        jax.ShapeDtypeStruct((seg, batch * S5_W), F32),
    )
    out_specs = (
        pl.BlockSpec((tm, d), lambda i: (i, 0)),
        pl.BlockSpec((N_Q_HEADS * LANES, tm), lambda i: (0, i)),
        pl.BlockSpec((cpt, KV_CHUNK, KV_W), lambda i: (i, 0, 0)),
        pl.BlockSpec((cpt, N_KV_HEADS, V_ROWS, KV_CHUNK), lambda i: (i, 0, 0, 0)),
        pl.BlockSpec((tm, POOL_W), lambda i: (i, 0)),
        pl.BlockSpec((tm, S5_W), lambda i: (i % tpb, i // tpb)),
    )
    return pl.pallas_call(
        functools.partial(_ffn_inproj_kernel, rope=rope),
        grid=(n // tm,),
        in_specs=in_specs,
        out_specs=out_specs,
        out_shape=out_shape,
        compiler_params=_params(("arbitrary",)),
        name="ffn_inproj",
    )(*args)


def _attn_kernel(*refs, n_lat):
    if n_lat:
        q_ref, kc_ref, vc_ref, kl_ref, vl_ref, o_ref, m_ref, acc_ref, ot_ref, sa_ref, sb_ref = refs
    else:
        q_ref, kc_ref, vc_ref, o_ref, m_ref, acc_ref, ot_ref, sa_ref, sb_ref = refs
    m_ref[...] = jnp.full(m_ref.shape, -1e30, F32)
    acc_ref[...] = jnp.zeros(acc_ref.shape, F32)

    def step(v_of_head, cur_ref, k_next, nxt_ref):
        for h in range(N_Q_HEADS):
            if k_next is not None:
                nxt_ref[h] = _dot(k_next, q_ref[h * LANES:(h + 1) * LANES, :])
            m_prev = m_ref[h]
            m_new = jnp.maximum(m_prev, jnp.max(cur_ref[h], axis=0, keepdims=True))
            alpha = jnp.exp2(m_prev - m_new)
            m_ref[h] = m_new
            pt = jnp.exp2(cur_ref[h] - m_new).astype(BF16)
            acc_ref[h] = alpha * acc_ref[h] + _dot(v_of_head(h // Q_PER_KV), pt)

    kc = kc_ref[0]
    for h in range(N_Q_HEADS):
        sa_ref[h] = _dot(kc, q_ref[h * LANES:(h + 1) * LANES, :])
    ctx_v = lambda g: vc_ref[0, g]
    if not n_lat:
        step(ctx_v, sa_ref, None, None)
    else:
        lat_v = lambda j: (lambda g: vl_ref[j, g])
        step(ctx_v, sa_ref, kl_ref[0], sb_ref)

        def body(i, carry):
            step(lat_v(2 * i), sb_ref, kl_ref[2 * i + 1], sa_ref)
            step(lat_v(2 * i + 1), sa_ref, kl_ref[2 * i + 2], sb_ref)
            return carry

        pairs = (n_lat - 1) // 2
        lax.fori_loop(0, pairs, body, 0)
        if n_lat - 2 * pairs == 2:
            step(lat_v(n_lat - 2), sb_ref, kl_ref[n_lat - 1], sa_ref)
            step(lat_v(n_lat - 1), sa_ref, None, None)
        else:
            step(lat_v(n_lat - 1), sb_ref, None, None)
    for h in range(N_Q_HEADS):
        ot_ref[h * HEAD_DIM:(h + 1) * HEAD_DIM, :] = (
            acc_ref[h, 0:HEAD_DIM, :] / acc_ref[h, HEAD_DIM:HEAD_DIM + 1, :])
    o_ref[...] = ot_ref[...].T


def _attention(qt, k_ctx, vt_ctx, k_lat, vt_lat, batch, tq):
    n = qt.shape[1]
    nq = n // batch // tq
    n_lat = 0 if k_lat is None else k_lat.shape[0] // batch
    in_specs = [
        pl.BlockSpec((N_Q_HEADS * LANES, tq), lambda b, i: (0, b * nq + i)),
        pl.BlockSpec((1, KV_CHUNK, KV_W), lambda b, i: (b, 0, 0)),
        pl.BlockSpec((1, N_KV_HEADS, V_ROWS, KV_CHUNK), lambda b, i: (b, 0, 0, 0)),
    ]
    args = [qt, k_ctx, vt_ctx]
    if n_lat:
        in_specs += [
            pl.BlockSpec((n_lat, KV_CHUNK, KV_W), lambda b, i: (b, 0, 0)),
            pl.BlockSpec((n_lat, N_KV_HEADS, V_ROWS, KV_CHUNK), lambda b, i: (b, 0, 0, 0)),
        ]
        args += [k_lat, vt_lat]
    return pl.pallas_call(
        functools.partial(_attn_kernel, n_lat=n_lat),
        grid=(batch, nq),
        in_specs=in_specs,
        out_specs=pl.BlockSpec((tq, ATTN_W), lambda b, i: (b * nq + i, 0)),
        out_shape=jax.ShapeDtypeStruct((n, ATTN_W), F32),
        scratch_shapes=[
            pltpu.VMEM((N_Q_HEADS, 1, tq), F32),
            pltpu.VMEM((N_Q_HEADS, V_ROWS, tq), F32),
            pltpu.VMEM((ATTN_W, tq), F32),
            pltpu.VMEM((N_Q_HEADS, KV_CHUNK, tq), F32),
            pltpu.VMEM((N_Q_HEADS, KV_CHUNK, tq), F32),
        ],
        compiler_params=_params(("arbitrary", "arbitrary")),
        name="attention",
    )(*args)


def _s5_kernel(ufl_ref, ufc_ref, ubl_ref, ubc_ref, a_ref, bblk_ref, cblk_ref, yf_ref, yb_ref,
               sf_ref, sb_ref, st_ref, *, tc, batch, n_ctx_tiles, n_tiles):
    nch = S5_CHANNELS
    i = pl.program_id(0)

    @pl.when(i == 0)
    def _():
        st_ref[...] = jnp.zeros(st_ref.shape, F32)

    assert 2 * batch == SUBLANES
    ngrp = tc * batch // SUBLANES
    uf = jnp.where(i < n_ctx_tiles, ufc_ref[...], ufl_ref[...])
    ub = jnp.where(i < n_ctx_tiles, ubc_ref[...], ubl_ref[...])
    sf_ref[...] = _dot(uf.astype(BF16), bblk_ref[0]).reshape(ngrp, SUBLANES, 2 * nch)
    sb_ref[...] = _dot(ub.astype(BF16), bblk_ref[1]).reshape(ngrp, SUBLANES, 2 * nch)
    a_re, a_im = a_ref[0], a_ref[1]
    top = lax.broadcasted_iota(jnp.int32, (SUBLANES, 1), 0) < batch

    def advance(s_re, s_im, x):
        return (a_re * s_re - a_im * s_im + x[:, :nch], a_re * s_im + a_im * s_re + x[:, nch:])

    def body(g, carry):
        s_re, s_im = carry
        gb = ngrp - 1 - g
        f = sf_ref[g]
        b = sb_ref[gb]
        first = jnp.where(top, f, b)
        second = pltpu.roll(jnp.where(top, b, f), batch, 0)
        r1, i1 = advance(s_re, s_im, first)
        r2, i2 = advance(r1, i1, second)
        s1 = jnp.concatenate([r1, i1], axis=1)
        s2 = pltpu.roll(jnp.concatenate([r2, i2], axis=1), batch, 0)
        sf_ref[g] = jnp.where(top, s1, s2)
        sb_ref[gb] = jnp.where(top, s2, s1)
        return r2, i2

    s_re, s_im = lax.fori_loop(0, ngrp, body, (st_ref[0], st_ref[1]))
    st_ref[0] = s_re
    st_ref[1] = s_im
    rows = ngrp * SUBLANES
    yf_ref[...] = _dot(sf_ref[...].reshape(rows, 2 * nch).astype(BF16), cblk_ref[0])
    yb_ref[...] = _dot(sb_ref[...].reshape(rows, 2 * nch).astype(BF16), cblk_ref[1])


def _s5_scan(u_lat, u_ctx, avec, bblk, cblk, l, batch, tc):
    rows = tc * batch
    n_lat_tiles = u_lat.shape[0] // rows
    n_ctx_tiles = u_ctx.shape[0] // rows
    n_tiles = n_lat_tiles + n_ctx_tiles
    nch = S5_CHANNELS

    def fwd(i):
        return jnp.where(i < n_ctx_tiles, n_lat_tiles + i, i - n_ctx_tiles)

    def bwd(i):
        return n_tiles - 1 - i

    def lat_of(tile):
        return lambda i: (jnp.clip(tile(i), 0, n_lat_tiles - 1), 0)

    def ctx_of(tile):
        return lambda i: (jnp.clip(tile(i) - n_lat_tiles, 0, n_ctx_tiles - 1), 0)

    blk = (rows, S5_W)
    shp = jax.ShapeDtypeStruct((n_tiles * rows, S5_W), F32)
    return pl.pallas_call(
        functools.partial(_s5_kernel, tc=tc, batch=batch, n_ctx_tiles=n_ctx_tiles, n_tiles=n_tiles),
        grid=(n_tiles,),
        in_specs=[
            pl.BlockSpec(blk, lat_of(fwd)), pl.BlockSpec(blk, ctx_of(fwd)),
            pl.BlockSpec(blk, lat_of(bwd)), pl.BlockSpec(blk, ctx_of(bwd)),
            pl.BlockSpec((None, 2, SUBLANES, nch), lambda i: (l, 0, 0, 0)),
            pl.BlockSpec((None, 2, S5_W, 2 * nch), lambda i: (l, 0, 0, 0)),
            pl.BlockSpec((None, 2, 2 * nch, S5_W), lambda i: (l, 0, 0, 0)),
        ],
        out_specs=(
            pl.BlockSpec(blk, lambda i: (fwd(i), 0)),
            pl.BlockSpec(blk, lambda i: (bwd(i), 0)),
        ),
        out_shape=(shp, shp),
        scratch_shapes=[
            pltpu.VMEM((rows // SUBLANES, SUBLANES, 2 * nch), F32),
            pltpu.VMEM((rows // SUBLANES, SUBLANES, 2 * nch), F32),
            pltpu.VMEM((2, SUBLANES, nch), F32),
        ],
        compiler_params=_params(("arbitrary",)),
        name="s5_scan",
    )(u_lat, u_ctx, u_lat, u_ctx, avec, bblk, cblk)


def _gelu_tanh(x):
    return 0.5 * x * (1.0 + jnp.tanh(math.sqrt(2.0 / math.pi) * (x + 0.044715 * (x * x * x))))


def _merge_ffn_kernel(x_ref, mod1_ref, attn_ref, pin_ref, prev_ref, next_ref, u_ref, yf_ref, yb_ref,
                      an_ref, sn_ref, pw_ref, ps_ref, dsk_ref, gw_ref, gb_ref, wo_ref,
                      mod2_ref, g2_ref, wg_ref, wu_ref, wd_ref, *rest, tm, tpb, seg, final):
    if final:
        fg_ref, o_ref, ext_ref = rest
    else:
        o_ref, ext_ref = rest
    ti = pl.program_id(0) % tpb
    a_n = _rmsnorm(attn_ref[...], an_ref[...])

    hal = POOL_HALO
    ext_ref[pl.ds(0, hal), :] = jnp.where(ti > 0, prev_ref[...], 0.0)
    ext_ref[pl.ds(hal, tm), :] = pin_ref[...]
    ext_ref[pl.ds(hal + tm, hal), :] = jnp.where(ti < tpb - 1, next_ref[...], 0.0)
    t = ti * tm + lax.broadcasted_iota(jnp.int32, (tm, 1), 0)
    low = lax.broadcasted_iota(jnp.int32, (1, LANES), 1) < POOL_CH

    def shifted(o, c):
        return ext_ref[pl.ds(hal + o, tm), pl.ds(c * LANES, LANES)]

    def count(w):
        hi = jnp.minimum(t + (w - w // 2), seg)
        lo = jnp.maximum(t - w // 2, 0)
        return (hi - lo).astype(F32)

    def window_sums(c, w_small, w_big):
        acc = shifted(0, c)
        for o in range(-(w_small // 2), w_small - w_small // 2):
            if o != 0:
                acc = acc + shifted(o, c)
        small = acc
        for o in range(-(w_big // 2), w_big - w_big // 2):
            if not (-(w_small // 2) <= o < w_small - w_small // 2):
                acc = acc + shifted(o, c)
        return jnp.where(low, small / count(w_small), acc / count(w_big))

    pooled = jnp.concatenate(
        [window_sums(0, POOL_WINDOWS[0], POOL_WINDOWS[1]),
         window_sums(1, POOL_WINDOWS[2], POOL_WINDOWS[3])], axis=-1) - pin_ref[...]
    p_out = _dot(pooled.astype(BF16), pw_ref[...]) * ps_ref[...]

    y = dsk_ref[...] * u_ref[...] + yf_ref[...] + yb_ref[...]
    g = _gelu_tanh(y)
    glu = g * jax.nn.sigmoid(_dot(g.astype(BF16), gw_ref[...]) + gb_ref[...])
    s_n = _rmsnorm(glu, sn_ref[...])

    cat = jnp.concatenate([a_n, p_out, s_n], axis=-1).astype(BF16)
    x = x_ref[...] + mod1_ref[2] * _dot(cat, wo_ref[...])
    x = _ffn_compute(x, mod2_ref, g2_ref, wg_ref, wu_ref, wd_ref)
    if final:
        x = _rmsnorm(x, fg_ref[...])
    o_ref[...] = x


def _merge_ffn(x, mods, norm_sub, attn, pin, u_tm, yf_tm, yb_tm, y_row_off, an, sn, pw, ps, dsk, gw, gb, wo,
               wg, wu, wd, l, row_fn, tm, batch, final_g=None):
    n, d = x.shape
    seg = n // batch
    tpb = seg // tm
    hal = POOL_HALO
    nhb = n // hal
    yo = y_row_off // tm
    single = pl.Buffered(1)

    def tmaj(off):
        return pl.BlockSpec((tm, S5_W), lambda i: (off + i % tpb, i // tpb))

    def vec(width):
        return pl.BlockSpec((None, 1, width), lambda i: (l, 0, 0))

    def mat(r, c):
        return pl.BlockSpec((None, r, c), lambda i: (l, 0, 0), pipeline_mode=single)

    in_specs = [
        pl.BlockSpec((tm, d), lambda i: (i, 0)),
        _mod_spec(l, row_fn, 1),
        pl.BlockSpec((tm, ATTN_W), lambda i: (i, 0)),
        pl.BlockSpec((tm, POOL_W), lambda i: (i, 0)),
        pl.BlockSpec((hal, POOL_W), lambda i: (jnp.maximum(i * (tm // hal) - 1, 0), 0)),
        pl.BlockSpec((hal, POOL_W), lambda i: (jnp.minimum((i + 1) * (tm // hal), nhb - 1), 0)),
        tmaj(0), tmaj(yo), tmaj(yo),
        vec(ATTN_W), vec(S5_W), mat(POOL_W, POOL_W), vec(POOL_W), vec(S5_W), mat(S5_W, S5_W), vec(S5_W),
        mat(d, d),
        _mod_spec(l, row_fn, 2),
    ] + _ffn_specs(l, 2, 1)
    args = [x, mods, attn, pin, pin, pin, u_tm, yf_tm, yb_tm, an, sn, pw, ps, dsk, gw, gb, wo,
            mods, norm_sub, wg, wu, wd]
    if final_g is not None:
        in_specs.append(pl.BlockSpec((1, d), lambda i: (0, 0)))
        args.append(final_g.reshape(1, d))
    return pl.pallas_call(
        functools.partial(_merge_ffn_kernel, tm=tm, tpb=tpb, seg=seg, final=final_g is not None),
        grid=(n // tm,),
        in_specs=in_specs,
        out_specs=pl.BlockSpec((tm, d), lambda i: (i, 0)),
        out_shape=jax.ShapeDtypeStruct((n, d), F32),
        scratch_shapes=[pltpu.VMEM((tm + 2 * hal, POOL_W), F32)],
        compiler_params=_params(("arbitrary",)),
        name="merge_ffn",
    )(*args)


def _rope_tables(seq):
    rows = seq // GRID_W
    row = jnp.broadcast_to(jnp.arange(rows, dtype=F32)[:, None], (rows, GRID_W)).reshape(-1)
    col = jnp.broadcast_to(jnp.arange(GRID_W, dtype=F32)[None, :], (rows, GRID_W)).reshape(-1)
    inv = ROPE_THETA ** (-2.0 * jnp.arange(AXIS_FREQS, dtype=F32) / AXIS_DIM)
    ang = jnp.concatenate([row[:, None] * inv, col[:, None] * inv], axis=-1)
    cos, sin = jnp.cos(ang), jnp.sin(ang)
    dim = jnp.arange(LANES) % HEAD_DIM
    idx = (dim // AXIS_DIM) * AXIS_FREQS + dim % AXIS_FREQS
    sign = jnp.where((dim % AXIS_DIM) < AXIS_FREQS, -1.0, 1.0).astype(F32)
    return cos[:, idx], sin[:, idx] * sign


def _block_diag(w):
    dep, g, c, c2 = w.shape
    eye = jnp.eye(g, dtype=w.dtype)
    return jnp.einsum('gk,lgcd->lgckd', eye, w).reshape(dep, g * c, g * c2)


def kernel(x, c, ctx, c_ctx, w_ada, b_ada, norm_sub, w_ffn_gate, w_ffn_up, w_ffn_down, w_in, w_out, q_norm, k_norm, attn_out_norm, ssm_out_norm, pool_w, pool_scale, s5_lam_re, s5_lam_im, s5_log_dt, s5_b_re, s5_b_im, s5_c_re, s5_c_im, s5_d, s5_glu_w, s5_glu_b, final_norm):
    batch, seq, d = x.shape
    ctx_len = ctx.shape[1]
    tm_lat, tm_ctx = 512, 256
    tq = 512
    tc = 256
    lat_tpb = seq // tm_lat

    cond = jnp.concatenate([c_ctx[None], c, jnp.zeros((SUBLANES - 1 - batch, d), F32)], axis=0)
    mods = _adaln(cond, w_ada, b_ada).reshape(DEPTH, SUBLANES, 3, 3, 1, d)
    lat_row = lambda i: 1 + i // lat_tpb
    ctx_row = lambda i: 0

    nsub = norm_sub.reshape(DEPTH, 3, 1, d)
    wg, wu, wd = w_ffn_gate.astype(BF16), w_ffn_up.astype(BF16), w_ffn_down.astype(BF16)
    w_in_b, w_out_b = w_in.astype(BF16), w_out.astype(BF16)
    qn = jnp.tile(q_norm, (1, LANES // HEAD_DIM)).reshape(DEPTH, 1, LANES)
    kn = jnp.tile(k_norm, (1, LANES // HEAD_DIM)).reshape(DEPTH, 1, LANES)
    an = attn_out_norm.reshape(DEPTH, 1, ATTN_W)
    sn = ssm_out_norm.reshape(DEPTH, 1, S5_W)
    pw = _block_diag(pool_w).astype(BF16)
    ps = pool_scale.reshape(DEPTH, 1, POOL_W)
    dsk = s5_d.reshape(DEPTH, 1, S5_W)
    gw = _block_diag(s5_glu_w).astype(BF16)
    gb = s5_glu_b.reshape(DEPTH, 1, S5_W)
    avec, bblk, cblk = _s5_tables(s5_lam_re, s5_lam_im, s5_log_dt, s5_b_re, s5_b_im, s5_c_re, s5_c_im)
    rope_tabs = _rope_tables(seq)

    xs = x.reshape(batch * seq, d)
    zs = ctx.reshape(batch * ctx_len, d)
    for l in range(DEPTH):
        last = l == DEPTH - 1
        xs, qx, kx, vx, px, ux = _ffn_inproj(xs, mods, nsub, wg, wu, wd, w_in_b, qn, kn, l, lat_row,
                                             tm_lat, batch, rope_tabs)
        zs, qz, kz, vz, pz, uz = _ffn_inproj(zs, mods, nsub, wg, wu, wd, w_in_b, qn, kn, l, ctx_row,
                                             tm_ctx, batch, None)

        attn_x = _attention(qx, kz, vz, kx, vx, batch, tq)

        yf, yb = _s5_scan(ux.reshape(seq * batch, S5_W), uz.reshape(ctx_len * batch, S5_W),
                          avec, bblk, cblk, l, batch, tc)
        yf = yf.reshape(seq + ctx_len, batch * S5_W)
        yb = yb.reshape(seq + ctx_len, batch * S5_W)

        xs = _merge_ffn(xs, mods, nsub, attn_x, px, ux, yf, yb, 0, an, sn, pw, ps, dsk, gw, gb, w_out_b,
                        wg, wu, wd, l, lat_row, tm_lat, batch, final_g=final_norm if last else None)
        if not last:
            attn_z = _attention(qz, kz, vz, None, None, batch, tm_ctx)
            zs = _merge_ffn(zs, mods, nsub, attn_z, pz, uz, yf, yb, seq, an, sn, pw, ps, dsk, gw, gb,
                            w_out_b, wg, wu, wd, l, ctx_row, tm_ctx, batch)
    return xs.reshape(batch, seq, d)
```

```python
import functools
import math

import jax
import jax.numpy as jnp
from jax import lax
from jax.experimental import pallas as pl
from jax.experimental.pallas import tpu as pltpu

F32 = jnp.float32
BF16 = jnp.bfloat16

D_MODEL = 1024
DEPTH = 4
GRID_W = 64
N_MOD = 9
EPS = 1e-6
D_FF = 2816
ATTN_W = 512
POOL_W = 256
S5_W = 256
HEAD_DIM = 64
N_Q_HEADS = 8
N_KV_HEADS = 2
Q_PER_KV = 4
KV_W = 128
ROPE_THETA = 10000.0
AXIS_DIM = 32
AXIS_FREQS = 16
POOL_WINDOWS = (2, 4, 8, 16)
POOL_CH = 64
S5_CH = 16
S5_GROUPS = 16
S5_STATE = 64
S5_CHANNELS = S5_GROUPS * S5_STATE
IN_W = ATTN_W + 2 * KV_W + POOL_W + S5_W

LANES = 128
SUBLANES = 8
MXU_DIM = 256
POOL_HALO = 8
KV_CHUNK = MXU_DIM
V_ROWS = HEAD_DIM + 16
FF_SPLITS = (0, 6 * MXU_DIM, D_FF)
VMEM_LIMIT = 58 * 1024 * 1024


def _params(sem):
    return pltpu.CompilerParams(dimension_semantics=sem, vmem_limit_bytes=VMEM_LIMIT)


def _dot(a, b):
    return jnp.dot(a, b, preferred_element_type=F32)


def _silu(x):
    return x * jax.nn.sigmoid(x)


def _rmsnorm(x, g):
    return x * lax.rsqrt(jnp.mean(x * x, axis=-1, keepdims=True) + EPS) * g


def _modulate(x, g, shift, scale):
    return _rmsnorm(x, g) * (1.0 + scale) + shift


def _adaln_kernel(cond_ref, w_ref, b_ref, o_ref):
    s = _silu(cond_ref[...]).astype(BF16)
    o_ref[...] = _dot(s, w_ref[...].astype(BF16)) + b_ref[...]


def _adaln(cond, w_ada, b_ada):
    d = D_MODEL
    tn = 1024
    nj = (N_MOD * d) // tn
    return pl.pallas_call(
        _adaln_kernel,
        grid=(DEPTH, nj),
        in_specs=[
            pl.BlockSpec((SUBLANES, d), lambda l, j: (0, 0)),
            pl.BlockSpec((None, d, tn), lambda l, j: (l, 0, j)),
            pl.BlockSpec((None, 1, tn), lambda l, j: (l, 0, j)),
        ],
        out_specs=pl.BlockSpec((None, SUBLANES, tn), lambda l, j: (l, 0, j)),
        out_shape=jax.ShapeDtypeStruct((DEPTH, SUBLANES, N_MOD * d), F32),
        compiler_params=_params(("arbitrary", "arbitrary")),
        name="adaln",
    )(cond, w_ada, b_ada.reshape(DEPTH, 1, N_MOD * d))


def _zoh_kernel(lr_ref, li_ref, ldt_ref, br_ref, bi_ref, abr_ref, abi_ref, bbr_ref, bbi_ref):
    dt = jnp.exp(ldt_ref[...])
    lr = lr_ref[...]
    li = li_ref[...]
    mag = jnp.exp(lr * dt)
    ab_re = mag * jnp.cos(li * dt)
    ab_im = mag * jnp.sin(li * dt)
    den = lr * lr + li * li
    f_re = ((ab_re - 1.0) * lr + ab_im * li) / den
    f_im = (ab_im * lr - (ab_re - 1.0) * li) / den
    br = br_ref[...]
    bi = bi_ref[...]
    abr_ref[...] = ab_re
    abi_ref[...] = ab_im
    bbr_ref[...] = f_re * br - f_im * bi
    bbi_ref[...] = f_re * bi + f_im * br


def _s5_tables(lam_re, lam_im, log_dt, b_re, b_im, c_re, c_im):
    g, h, p = S5_GROUPS, S5_CH, S5_STATE
    rows = DEPTH * 2 * g * h

    def rep(a):
        return jnp.broadcast_to(a[:, :, :, None, :], (DEPTH, 2, g, h, p)).reshape(rows, p)

    ldt = jnp.broadcast_to(log_dt[:, :, :, None, None], (DEPTH, 2, g, h, p)).reshape(rows, p)
    bt_re = jnp.swapaxes(b_re, -1, -2).reshape(rows, p)
    bt_im = jnp.swapaxes(b_im, -1, -2).reshape(rows, p)
    shp = jax.ShapeDtypeStruct((rows, p), F32)
    ab_re, ab_im, bb_re, bb_im = pl.pallas_call(
        _zoh_kernel, out_shape=(shp, shp, shp, shp), name="s5_zoh",
    )(rep(lam_re), rep(lam_im), ldt, bt_re, bt_im)

    def chan(a):
        return a.reshape(DEPTH, 2, g, h, p)[:, :, :, 0, :].reshape(DEPTH, 2, g * p)

    half = SUBLANES // 2
    avec = jnp.stack([jnp.repeat(chan(ab_re), half, axis=1), jnp.repeat(chan(ab_im), half, axis=1)], axis=1)
    eye = jnp.eye(g, dtype=F32)

    def in_blk(bt):
        bt = bt.reshape(DEPTH, 2, g, h, p)
        return jnp.einsum('gk,ldghp->ldghkp', eye, bt).reshape(DEPTH, 2, g * h, g * p)

    def out_blk(cc):
        return jnp.einsum('gk,ldkhp->ldgpkh', eye, cc).reshape(DEPTH, 2, g * p, g * h)

    bblk = jnp.concatenate([in_blk(bb_re), in_blk(bb_im)], axis=-1).astype(BF16)
    cblk = jnp.concatenate([out_blk(c_re), out_blk(-c_im)], axis=-2).astype(BF16)
    return avec, bblk, cblk


def _mod_spec(l, row_fn, sub):
    return pl.BlockSpec((None, None, None, 3, 1, D_MODEL), lambda i: (l, row_fn(i), sub, 0, 0, 0))


def _ffn_specs(l, sub, which):
    single = pl.Buffered(1)
    d = D_MODEL
    return [
        pl.BlockSpec((None, None, 1, d), lambda i: (l, sub, 0, 0)),
        pl.BlockSpec((None, None, d, D_FF), lambda i: (l, which, 0, 0), pipeline_mode=single),
        pl.BlockSpec((None, None, d, D_FF), lambda i: (l, which, 0, 0), pipeline_mode=single),
        pl.BlockSpec((None, None, D_FF, d), lambda i: (l, which, 0, 0), pipeline_mode=single),
    ]


def _ffn_compute(x, mod_ref, g_ref, wg_ref, wu_ref, wd_ref):
    half = x.shape[0] // 2
    xs = (x[:half], x[half:])
    hs = [_modulate(xh, g_ref[...], mod_ref[0], mod_ref[1]).astype(BF16) for xh in xs]
    ys = [None, None]
    for c0, c1 in zip(FF_SPLITS[:-1], FF_SPLITS[1:]):
        for r in range(2):
            a = _dot(hs[r], wg_ref[:, c0:c1])
            u = _dot(hs[r], wu_ref[:, c0:c1])
            part = _dot((_silu(a) * u).astype(BF16), wd_ref[c0:c1, :])
            ys[r] = part if ys[r] is None else ys[r] + part
    return jnp.concatenate([xs[r] + (0.5 * mod_ref[2]) * ys[r] for r in range(2)], axis=0)


def _ffn_inproj_kernel(x_ref, mod0_ref, g0_ref, wg_ref, wu_ref, wd_ref, mod1_ref, g1_ref, w_ref,
                       qn_ref, kn_ref, *rest, rope):
    if rope:
        cos_ref, sin_ref, y_ref, q_ref, k_ref, v_ref, p_ref, u_ref = rest
    else:
        y_ref, q_ref, k_ref, v_ref, p_ref, u_ref = rest
    x = _ffn_compute(x_ref[...], mod0_ref, g0_ref, wg_ref, wu_ref, wd_ref)
    y_ref[...] = x
    h = _modulate(x, g1_ref[...], mod1_ref[0], mod1_ref[1]).astype(BF16)
    proj = _dot(h, w_ref[...])

    lane = lax.broadcasted_iota(jnp.int32, (1, LANES), 1)
    low = lane < HEAD_DIM
    first_half = (lane & (AXIS_DIM - 1)) < AXIS_FREQS
    def head_norm(blk, w):
        width = blk.shape[1]
        er = lax.broadcasted_iota(jnp.int32, (width, width), 0) // HEAD_DIM
        ec = lax.broadcasted_iota(jnp.int32, (width, width), 1) // HEAD_DIM
        same_head = (er == ec).astype(BF16)
        sq = blk * blk
        hi = sq.astype(BF16)
        lo = (sq - hi.astype(F32)).astype(BF16)
        ss = _dot(hi, same_head) + _dot(lo, same_head)
        return blk * lax.rsqrt(ss * (1.0 / HEAD_DIM) + EPS) * w

    def rotary(y):
        if not rope:
            return y
        partner = jnp.where(first_half,
                            pltpu.roll(y, LANES - AXIS_FREQS, 1),
                            pltpu.roll(y, AXIS_FREQS, 1))
        return y * cos_ref[...] + partner * sin_ref[...]

    zero = jnp.zeros((), F32)
    q_scale = HEAD_DIM ** -0.5 * math.log2(math.e)
    qn2 = jnp.concatenate([qn_ref[...]] * (MXU_DIM // LANES), axis=-1)
    q_normed = [head_norm(proj[:, s * MXU_DIM:(s + 1) * MXU_DIM], qn2) for s in range(ATTN_W // MXU_DIM)]
    for j in range(N_Q_HEADS // 2):
        off = (j * LANES) % MXU_DIM
        y = rotary(q_normed[j * LANES // MXU_DIM][:, off:off + LANES]) * q_scale
        swapped = pltpu.roll(y, HEAD_DIM, 1)
        if j < N_Q_HEADS // 4:
            even, odd = jnp.where(low, y, zero), jnp.where(low, swapped, zero)
        else:
            even, odd = jnp.where(low, zero, swapped), jnp.where(low, zero, y)
        q_ref[(2 * j) * LANES:(2 * j + 1) * LANES, :] = even.T.astype(BF16)
        q_ref[(2 * j + 1) * LANES:(2 * j + 2) * LANES, :] = odd.T.astype(BF16)

    kblk = proj[:, ATTN_W:ATTN_W + KV_W]
    kk = rotary(head_norm(kblk, kn_ref[...])).astype(BF16)
    vblk = proj[:, ATTN_W + KV_W:ATTN_W + 2 * KV_W]
    ones = jnp.ones((V_ROWS - HEAD_DIM, KV_CHUNK), BF16)
    for c in range(k_ref.shape[0]):
        k_ref[c] = kk[c * KV_CHUNK:(c + 1) * KV_CHUNK, :]
        vt = vblk[c * KV_CHUNK:(c + 1) * KV_CHUNK, :].T.astype(BF16)
        for g in range(N_KV_HEADS):
            v_ref[c, g, 0:HEAD_DIM, :] = vt[g * HEAD_DIM:(g + 1) * HEAD_DIM, :]
            v_ref[c, g, HEAD_DIM:V_ROWS, :] = ones

    p_ref[...] = proj[:, ATTN_W + 2 * KV_W:ATTN_W + 2 * KV_W + POOL_W]
    u_ref[...] = proj[:, ATTN_W + 2 * KV_W + POOL_W:]


def _ffn_inproj(x, mods, norm_sub, wg, wu, wd, w_in, qn, kn, l, row_fn, tm, batch, rope_tabs):
    n, d = x.shape
    seg = n // batch
    tpb = seg // tm
    rope = rope_tabs is not None
    in_specs = (
        [pl.BlockSpec((tm, d), lambda i: (i, 0)), _mod_spec(l, row_fn, 0)]
        + _ffn_specs(l, 0, 0)
        + [_mod_spec(l, row_fn, 1),
           pl.BlockSpec((None, None, 1, d), lambda i: (l, 1, 0, 0)),
           pl.BlockSpec((None, d, IN_W), lambda i: (l, 0, 0), pipeline_mode=pl.Buffered(1)),
           pl.BlockSpec((None, 1, LANES), lambda i: (l, 0, 0)),
           pl.BlockSpec((None, 1, LANES), lambda i: (l, 0, 0))])
    args = [x, mods, norm_sub, wg, wu, wd, mods, norm_sub, w_in, qn, kn]
    if rope:
        in_specs += [pl.BlockSpec((tm, LANES), lambda i: (i % tpb, 0))] * 2
        args += list(rope_tabs)
    cpt = tm // KV_CHUNK
    out_shape = (
        jax.ShapeDtypeStruct((n, d), F32),
        jax.ShapeDtypeStruct((N_Q_HEADS * LANES, n), BF16),
        jax.ShapeDtypeStruct((n // KV_CHUNK, KV_CHUNK, KV_W), BF16),
        jax.ShapeDtypeStruct((n // KV_CHUNK, N_KV_HEADS, V_ROWS, KV_CHUNK), BF16),
        jax.ShapeDtypeStruct((n, POOL_W), F32),
        jax.ShapeDtypeStruct((seg, batch * S5_W), F32),
    )
    out_specs = (
        pl.BlockSpec((tm, d), lambda i: (i, 0)),
        pl.BlockSpec((N_Q_HEADS * LANES, tm), lambda i: (0, i)),
        pl.BlockSpec((cpt, KV_CHUNK, KV_W), lambda i: (i, 0, 0)),
        pl.BlockSpec((cpt, N_KV_HEADS, V_ROWS, KV_CHUNK), lambda i: (i, 0, 0, 0)),
        pl.BlockSpec((tm, POOL_W), lambda i: (i, 0)),
        pl.BlockSpec((tm, S5_W), lambda i: (i % tpb, i // tpb)),
    )
    return pl.pallas_call(
        functools.partial(_ffn_inproj_kernel, rope=rope),
        grid=(n // tm,),
        in_specs=in_specs,
        out_specs=out_specs,
        out_shape=out_shape,
        compiler_params=_params(("arbitrary",)),
        name="ffn_inproj",
    )(*args)


def _attn_kernel(*refs, n_lat):
    if n_lat:
        q_ref, kc_ref, vc_ref, kl_ref, vl_ref, o_ref, m_ref, acc_ref, ot_ref, sa_ref, sb_ref = refs
    else:
        q_ref, kc_ref, vc_ref, o_ref, m_ref, acc_ref, ot_ref, sa_ref, sb_ref = refs
    m_ref[...] = jnp.full(m_ref.shape, -1e30, F32)
    acc_ref[...] = jnp.zeros(acc_ref.shape, F32)

    def step(v_of_head, cur_ref, k_next, nxt_ref):
        for h in range(N_Q_HEADS):
            if k_next is not None and h % 2 == 0:
                for hh in (h, h + 1):
                    nxt_ref[hh] = _dot(k_next, q_ref[hh * LANES:(hh + 1) * LANES, :])
            m_prev = m_ref[h]
            m_new = jnp.maximum(m_prev, jnp.max(cur_ref[h], axis=0, keepdims=True))
            alpha = jnp.exp2(m_prev - m_new)
            m_ref[h] = m_new
            pt = jnp.exp2(cur_ref[h] - m_new).astype(BF16)
            acc_ref[h] = alpha * acc_ref[h] + _dot(v_of_head(h // Q_PER_KV), pt)

    kc = kc_ref[0]
    for h in range(N_Q_HEADS):
        sa_ref[h] = _dot(kc, q_ref[h * LANES:(h + 1) * LANES, :])
    ctx_v = lambda g: vc_ref[0, g]
    if not n_lat:
        step(ctx_v, sa_ref, None, None)
    else:
        lat_v = lambda j: (lambda g: vl_ref[j, g])
        step(ctx_v, sa_ref, kl_ref[0], sb_ref)

        def body(i, carry):
            step(lat_v(2 * i), sb_ref, kl_ref[2 * i + 1], sa_ref)
            step(lat_v(2 * i + 1), sa_ref, kl_ref[2 * i + 2], sb_ref)
            return carry

        pairs = (n_lat - 1) // 2
        lax.fori_loop(0, pairs, body, 0)
        if n_lat - 2 * pairs == 2:
            step(lat_v(n_lat - 2), sb_ref, kl_ref[n_lat - 1], sa_ref)
            step(lat_v(n_lat - 1), sa_ref, None, None)
        else:
            step(lat_v(n_lat - 1), sb_ref, None, None)
    for h in range(N_Q_HEADS):
        ot_ref[h * HEAD_DIM:(h + 1) * HEAD_DIM, :] = (
            acc_ref[h, 0:HEAD_DIM, :] / acc_ref[h, HEAD_DIM:HEAD_DIM + 1, :])
    o_ref[...] = ot_ref[...].T


def _attention(qt, k_ctx, vt_ctx, k_lat, vt_lat, batch, tq):
    n = qt.shape[1]
    nq = n // batch // tq
    n_lat = 0 if k_lat is None else k_lat.shape[0] // batch
    in_specs = [
        pl.BlockSpec((N_Q_HEADS * LANES, tq), lambda b, i: (0, b * nq + i)),
        pl.BlockSpec((1, KV_CHUNK, KV_W), lambda b, i: (b, 0, 0)),
        pl.BlockSpec((1, N_KV_HEADS, V_ROWS, KV_CHUNK), lambda b, i: (b, 0, 0, 0)),
    ]
    args = [qt, k_ctx, vt_ctx]
    if n_lat:
        in_specs += [
            pl.BlockSpec((n_lat, KV_CHUNK, KV_W), lambda b, i: (b, 0, 0)),
            pl.BlockSpec((n_lat, N_KV_HEADS, V_ROWS, KV_CHUNK), lambda b, i: (b, 0, 0, 0)),
        ]
        args += [k_lat, vt_lat]
    return pl.pallas_call(
        functools.partial(_attn_kernel, n_lat=n_lat),
        grid=(batch, nq),
        in_specs=in_specs,
        out_specs=pl.BlockSpec((tq, ATTN_W), lambda b, i: (b * nq + i, 0)),
        out_shape=jax.ShapeDtypeStruct((n, ATTN_W), F32),
        scratch_shapes=[
            pltpu.VMEM((N_Q_HEADS, 1, tq), F32),
            pltpu.VMEM((N_Q_HEADS, V_ROWS, tq), F32),
            pltpu.VMEM((ATTN_W, tq), F32),
            pltpu.VMEM((N_Q_HEADS, KV_CHUNK, tq), F32),
            pltpu.VMEM((N_Q_HEADS, KV_CHUNK, tq), F32),
        ],
        compiler_params=_params(("arbitrary", "arbitrary")),
        name="attention",
    )(*args)


def _s5_kernel(ufl_ref, ufc_ref, ubl_ref, ubc_ref, a_ref, bblk_ref, cblk_ref, yf_ref, yb_ref,
               sf_ref, sb_ref, st_ref, *, tc, batch, n_ctx_tiles, n_tiles):
    nch = S5_CHANNELS
    i = pl.program_id(0)

    @pl.when(i == 0)
    def _():
        st_ref[...] = jnp.zeros(st_ref.shape, F32)

    assert 2 * batch == SUBLANES
    ngrp = tc * batch // SUBLANES
    uf = jnp.where(i < n_ctx_tiles, ufc_ref[...], ufl_ref[...])
    ub = jnp.where(i < n_ctx_tiles, ubc_ref[...], ubl_ref[...])
    sf_ref[...] = _dot(uf.astype(BF16), bblk_ref[0]).reshape(ngrp, SUBLANES, 2 * nch)
    sb_ref[...] = _dot(ub.astype(BF16), bblk_ref[1]).reshape(ngrp, SUBLANES, 2 * nch)
    a_re, a_im = a_ref[0], a_ref[1]
    top = lax.broadcasted_iota(jnp.int32, (SUBLANES, 1), 0) < batch

    def advance(s_re, s_im, x):
        return (a_re * s_re - a_im * s_im + x[:, :nch], a_re * s_im + a_im * s_re + x[:, nch:])

    def body(g, carry):
        s_re, s_im = carry
        gb = ngrp - 1 - g
        f = sf_ref[g]
        b = sb_ref[gb]
        first = jnp.where(top, f, b)
        second = pltpu.roll(jnp.where(top, b, f), batch, 0)
        r1, i1 = advance(s_re, s_im, first)
        r2, i2 = advance(r1, i1, second)
        s1 = jnp.concatenate([r1, i1], axis=1)
        s2 = pltpu.roll(jnp.concatenate([r2, i2], axis=1), batch, 0)
        sf_ref[g] = jnp.where(top, s1, s2)
        sb_ref[gb] = jnp.where(top, s2, s1)
        return r2, i2

    s_re, s_im = lax.fori_loop(0, ngrp, body, (st_ref[0], st_ref[1]))
    st_ref[0] = s_re
    st_ref[1] = s_im
    rows = ngrp * SUBLANES
    yf_ref[...] = _dot(sf_ref[...].reshape(rows, 2 * nch).astype(BF16), cblk_ref[0])
    yb_ref[...] = _dot(sb_ref[...].reshape(rows, 2 * nch).astype(BF16), cblk_ref[1])


def _s5_scan(u_lat, u_ctx, avec, bblk, cblk, l, batch, tc):
    rows = tc * batch
    n_lat_tiles = u_lat.shape[0] // rows
    n_ctx_tiles = u_ctx.shape[0] // rows
    n_tiles = n_lat_tiles + n_ctx_tiles
    nch = S5_CHANNELS

    def fwd(i):
        return jnp.where(i < n_ctx_tiles, n_lat_tiles + i, i - n_ctx_tiles)

    def bwd(i):
        return n_tiles - 1 - i

    def lat_of(tile):
        return lambda i: (jnp.clip(tile(i), 0, n_lat_tiles - 1), 0)

    def ctx_of(tile):
        return lambda i: (jnp.clip(tile(i) - n_lat_tiles, 0, n_ctx_tiles - 1), 0)

    blk = (rows, S5_W)
    shp = jax.ShapeDtypeStruct((n_tiles * rows, S5_W), F32)
    return pl.pallas_call(
        functools.partial(_s5_kernel, tc=tc, batch=batch, n_ctx_tiles=n_ctx_tiles, n_tiles=n_tiles),
        grid=(n_tiles,),
        in_specs=[
            pl.BlockSpec(blk, lat_of(fwd)), pl.BlockSpec(blk, ctx_of(fwd)),
            pl.BlockSpec(blk, lat_of(bwd)), pl.BlockSpec(blk, ctx_of(bwd)),
            pl.BlockSpec((None, 2, SUBLANES, nch), lambda i: (l, 0, 0, 0)),
            pl.BlockSpec((None, 2, S5_W, 2 * nch), lambda i: (l, 0, 0, 0)),
            pl.BlockSpec((None, 2, 2 * nch, S5_W), lambda i: (l, 0, 0, 0)),
        ],
        out_specs=(
            pl.BlockSpec(blk, lambda i: (fwd(i), 0)),
            pl.BlockSpec(blk, lambda i: (bwd(i), 0)),
        ),
        out_shape=(shp, shp),
        scratch_shapes=[
            pltpu.VMEM((rows // SUBLANES, SUBLANES, 2 * nch), F32),
            pltpu.VMEM((rows // SUBLANES, SUBLANES, 2 * nch), F32),
            pltpu.VMEM((2, SUBLANES, nch), F32),
        ],
        compiler_params=_params(("arbitrary",)),
        name="s5_scan",
    )(u_lat, u_ctx, u_lat, u_ctx, avec, bblk, cblk)


def _gelu_tanh(x):
    return 0.5 * x * (1.0 + jnp.tanh(math.sqrt(2.0 / math.pi) * (x + 0.044715 * (x * x * x))))


def _merge_ffn_kernel(x_ref, mod1_ref, attn_ref, pin_ref, prev_ref, next_ref, u_ref, yf_ref, yb_ref,
                      an_ref, sn_ref, pw_ref, ps_ref, dsk_ref, gw_ref, gb_ref, wo_ref,
                      mod2_ref, g2_ref, wg_ref, wu_ref, wd_ref, *rest, tm, tpb, seg, final):
    if final:
        fg_ref, o_ref, ext_ref = rest
    else:
        o_ref, ext_ref = rest
    ti = pl.program_id(0) % tpb
    a_n = _rmsnorm(attn_ref[...], an_ref[...])

    hal = POOL_HALO
    ext_ref[pl.ds(0, hal), :] = jnp.where(ti > 0, prev_ref[...], 0.0)
    ext_ref[pl.ds(hal, tm), :] = pin_ref[...]
    ext_ref[pl.ds(hal + tm, hal), :] = jnp.where(ti < tpb - 1, next_ref[...], 0.0)
    t = ti * tm + lax.broadcasted_iota(jnp.int32, (tm, 1), 0)
    low = lax.broadcasted_iota(jnp.int32, (1, LANES), 1) < POOL_CH

    def shifted(o, c):
        return ext_ref[pl.ds(hal + o, tm), pl.ds(c * LANES, LANES)]

    def count(w):
        hi = jnp.minimum(t + (w - w // 2), seg)
        lo = jnp.maximum(t - w // 2, 0)
        return (hi - lo).astype(F32)

    def window_sums(c, w_small, w_big):
        acc = shifted(0, c)
        for o in range(-(w_small // 2), w_small - w_small // 2):
            if o != 0:
                acc = acc + shifted(o, c)
        small = acc
        for o in range(-(w_big // 2), w_big - w_big // 2):
            if not (-(w_small // 2) <= o < w_small - w_small // 2):
                acc = acc + shifted(o, c)
        return jnp.where(low, small / count(w_small), acc / count(w_big))

    pooled = jnp.concatenate(
        [window_sums(0, POOL_WINDOWS[0], POOL_WINDOWS[1]),
         window_sums(1, POOL_WINDOWS[2], POOL_WINDOWS[3])], axis=-1) - pin_ref[...]
    p_out = _dot(pooled.astype(BF16), pw_ref[...]) * ps_ref[...]

    y = dsk_ref[...] * u_ref[...] + yf_ref[...] + yb_ref[...]
    g = _gelu_tanh(y)
    glu = g * jax.nn.sigmoid(_dot(g.astype(BF16), gw_ref[...]) + gb_ref[...])
    s_n = _rmsnorm(glu, sn_ref[...])

    cat = jnp.concatenate([a_n, p_out, s_n], axis=-1).astype(BF16)
    x = x_ref[...] + mod1_ref[2] * _dot(cat, wo_ref[...])
    x = _ffn_compute(x, mod2_ref, g2_ref, wg_ref, wu_ref, wd_ref)
    if final:
        x = _rmsnorm(x, fg_ref[...])
    o_ref[...] = x


def _merge_ffn(x, mods, norm_sub, attn, pin, u_tm, yf_tm, yb_tm, y_row_off, an, sn, pw, ps, dsk, gw, gb, wo,
               wg, wu, wd, l, row_fn, tm, batch, final_g=None):
    n, d = x.shape
    seg = n // batch
    tpb = seg // tm
    hal = POOL_HALO
    nhb = n // hal
    yo = y_row_off // tm
    single = pl.Buffered(1)

    def tmaj(off):
        return pl.BlockSpec((tm, S5_W), lambda i: (off + i % tpb, i // tpb))

    def vec(width):
        return pl.BlockSpec((None, 1, width), lambda i: (l, 0, 0))

    def mat(r, c):
        return pl.BlockSpec((None, r, c), lambda i: (l, 0, 0), pipeline_mode=single)

    in_specs = [
        pl.BlockSpec((tm, d), lambda i: (i, 0)),
        _mod_spec(l, row_fn, 1),
        pl.BlockSpec((tm, ATTN_W), lambda i: (i, 0)),
        pl.BlockSpec((tm, POOL_W), lambda i: (i, 0)),
        pl.BlockSpec((hal, POOL_W), lambda i: (jnp.maximum(i * (tm // hal) - 1, 0), 0)),
        pl.BlockSpec((hal, POOL_W), lambda i: (jnp.minimum((i + 1) * (tm // hal), nhb - 1), 0)),
        tmaj(0), tmaj(yo), tmaj(yo),
        vec(ATTN_W), vec(S5_W), mat(POOL_W, POOL_W), vec(POOL_W), vec(S5_W), mat(S5_W, S5_W), vec(S5_W),
        mat(d, d),
        _mod_spec(l, row_fn, 2),
    ] + _ffn_specs(l, 2, 1)
    args = [x, mods, attn, pin, pin, pin, u_tm, yf_tm, yb_tm, an, sn, pw, ps, dsk, gw, gb, wo,
            mods, norm_sub, wg, wu, wd]
    if final_g is not None:
        in_specs.append(pl.BlockSpec((1, d), lambda i: (0, 0)))
        args.append(final_g.reshape(1, d))
    return pl.pallas_call(
        functools.partial(_merge_ffn_kernel, tm=tm, tpb=tpb, seg=seg, final=final_g is not None),
        grid=(n // tm,),
        in_specs=in_specs,
        out_specs=pl.BlockSpec((tm, d), lambda i: (i, 0)),
        out_shape=jax.ShapeDtypeStruct((n, d), F32),
        scratch_shapes=[pltpu.VMEM((tm + 2 * hal, POOL_W), F32)],
        compiler_params=_params(("arbitrary",)),
        name="merge_ffn",
    )(*args)


def _rope_tables(seq):
    rows = seq // GRID_W
    row = jnp.broadcast_to(jnp.arange(rows, dtype=F32)[:, None], (rows, GRID_W)).reshape(-1)
    col = jnp.broadcast_to(jnp.arange(GRID_W, dtype=F32)[None, :], (rows, GRID_W)).reshape(-1)
    inv = ROPE_THETA ** (-2.0 * jnp.arange(AXIS_FREQS, dtype=F32) / AXIS_DIM)
    ang = jnp.concatenate([row[:, None] * inv, col[:, None] * inv], axis=-1)
    cos, sin = jnp.cos(ang), jnp.sin(ang)
    dim = jnp.arange(LANES) % HEAD_DIM
    idx = (dim // AXIS_DIM) * AXIS_FREQS + dim % AXIS_FREQS
    sign = jnp.where((dim % AXIS_DIM) < AXIS_FREQS, -1.0, 1.0).astype(F32)
    return cos[:, idx], sin[:, idx] * sign


def _block_diag(w):
    dep, g, c, c2 = w.shape
    eye = jnp.eye(g, dtype=w.dtype)
    return jnp.einsum('gk,lgcd->lgckd', eye, w).reshape(dep, g * c, g * c2)


def kernel(x, c, ctx, c_ctx, w_ada, b_ada, norm_sub, w_ffn_gate, w_ffn_up, w_ffn_down, w_in, w_out, q_norm, k_norm, attn_out_norm, ssm_out_norm, pool_w, pool_scale, s5_lam_re, s5_lam_im, s5_log_dt, s5_b_re, s5_b_im, s5_c_re, s5_c_im, s5_d, s5_glu_w, s5_glu_b, final_norm):
    batch, seq, d = x.shape
    ctx_len = ctx.shape[1]
    tm_lat, tm_ctx = 512, 256
    tq = 512
    tc = 256
    lat_tpb = seq // tm_lat

    cond = jnp.concatenate([c_ctx[None], c, jnp.zeros((SUBLANES - 1 - batch, d), F32)], axis=0)
    mods = _adaln(cond, w_ada, b_ada).reshape(DEPTH, SUBLANES, 3, 3, 1, d)
    lat_row = lambda i: 1 + i // lat_tpb
    ctx_row = lambda i: 0

    nsub = norm_sub.reshape(DEPTH, 3, 1, d)
    wg, wu, wd = w_ffn_gate.astype(BF16), w_ffn_up.astype(BF16), w_ffn_down.astype(BF16)
    w_in_b, w_out_b = w_in.astype(BF16), w_out.astype(BF16)
    qn = jnp.tile(q_norm, (1, LANES // HEAD_DIM)).reshape(DEPTH, 1, LANES)
    kn = jnp.tile(k_norm, (1, LANES // HEAD_DIM)).reshape(DEPTH, 1, LANES)
    an = attn_out_norm.reshape(DEPTH, 1, ATTN_W)
    sn = ssm_out_norm.reshape(DEPTH, 1, S5_W)
    pw = _block_diag(pool_w).astype(BF16)
    ps = pool_scale.reshape(DEPTH, 1, POOL_W)
    dsk = s5_d.reshape(DEPTH, 1, S5_W)
    gw = _block_diag(s5_glu_w).astype(BF16)
    gb = s5_glu_b.reshape(DEPTH, 1, S5_W)
    avec, bblk, cblk = _s5_tables(s5_lam_re, s5_lam_im, s5_log_dt, s5_b_re, s5_b_im, s5_c_re, s5_c_im)
    rope_tabs = _rope_tables(seq)

    xs = x.reshape(batch * seq, d)
    zs = ctx.reshape(batch * ctx_len, d)
    for l in range(DEPTH):
        last = l == DEPTH - 1
        xs, qx, kx, vx, px, ux = _ffn_inproj(xs, mods, nsub, wg, wu, wd, w_in_b, qn, kn, l, lat_row,
                                             tm_lat, batch, rope_tabs)
        zs, qz, kz, vz, pz, uz = _ffn_inproj(zs, mods, nsub, wg, wu, wd, w_in_b, qn, kn, l, ctx_row,
                                             tm_ctx, batch, None)

        attn_x = _attention(qx, kz, vz, kx, vx, batch, tq)

        yf, yb = _s5_scan(ux.reshape(seq * batch, S5_W), uz.reshape(ctx_len * batch, S5_W),
                          avec, bblk, cblk, l, batch, tc)
        yf = yf.reshape(seq + ctx_len, batch * S5_W)
        yb = yb.reshape(seq + ctx_len, batch * S5_W)

        xs = _merge_ffn(xs, mods, nsub, attn_x, px, ux, yf, yb, 0, an, sn, pw, ps, dsk, gw, gb, w_out_b,
                        wg, wu, wd, l, lat_row, tm_lat, batch, final_g=final_norm if last else None)
        if not last:
            attn_z = _attention(qz, kz, vz, None, None, batch, tm_ctx)
            zs = _merge_ffn(zs, mods, nsub, attn_z, pz, uz, yf, yb, seq, an, sn, pw, ps, dsk, gw, gb,
                            w_out_b, wg, wu, wd, l, ctx_row, tm_ctx, batch)
    return xs.reshape(batch, seq, d)
```

```python
import functools
import math

import jax
import jax.numpy as jnp
from jax import lax
from jax.experimental import pallas as pl
from jax.experimental.pallas import tpu as pltpu

F32 = jnp.float32
BF16 = jnp.bfloat16

D_MODEL = 1024
DEPTH = 4
GRID_W = 64
N_MOD = 9
EPS = 1e-6
D_FF = 2816
ATTN_W = 512
POOL_W = 256
S5_W = 256
HEAD_DIM = 64
N_Q_HEADS = 8
N_KV_HEADS = 2
Q_PER_KV = 4
KV_W = 128
ROPE_THETA = 10000.0
AXIS_DIM = 32
AXIS_FREQS = 16
POOL_WINDOWS = (2, 4, 8, 16)
POOL_CH = 64
S5_CH = 16
S5_GROUPS = 16
S5_STATE = 64
S5_CHANNELS = S5_GROUPS * S5_STATE
IN_W = ATTN_W + 2 * KV_W + POOL_W + S5_W

LANES = 128
SUBLANES = 8
MXU_DIM = 256
POOL_HALO = 8
KV_CHUNK = MXU_DIM
V_ROWS = HEAD_DIM + 16
FF_SPLITS = (0, 6 * MXU_DIM, D_FF)
VMEM_LIMIT = 58 * 1024 * 1024


def _params(sem):
    return pltpu.CompilerParams(dimension_semantics=sem, vmem_limit_bytes=VMEM_LIMIT)


def _dot(a, b):
    return jnp.dot(a, b, preferred_element_type=F32)


def _silu(x):
    return x * jax.nn.sigmoid(x)


def _rmsnorm(x, g):
    return x * lax.rsqrt(jnp.mean(x * x, axis=-1, keepdims=True) + EPS) * g


def _modulate(x, g, shift, scale):
    return _rmsnorm(x, g) * (1.0 + scale) + shift


def _adaln_kernel(cond_ref, w_ref, b_ref, o_ref):
    s = _silu(cond_ref[...]).astype(BF16)
    o_ref[...] = _dot(s, w_ref[...].astype(BF16)) + b_ref[...]


def _adaln(cond, w_ada, b_ada):
    d = D_MODEL
    tn = 1024
    nj = (N_MOD * d) // tn
    return pl.pallas_call(
        _adaln_kernel,
        grid=(DEPTH, nj),
        in_specs=[
            pl.BlockSpec((SUBLANES, d), lambda l, j: (0, 0)),
            pl.BlockSpec((None, d, tn), lambda l, j: (l, 0, j)),
            pl.BlockSpec((None, 1, tn), lambda l, j: (l, 0, j)),
        ],
        out_specs=pl.BlockSpec((None, SUBLANES, tn), lambda l, j: (l, 0, j)),
        out_shape=jax.ShapeDtypeStruct((DEPTH, SUBLANES, N_MOD * d), F32),
        compiler_params=_params(("arbitrary", "arbitrary")),
        name="adaln",
    )(cond, w_ada, b_ada.reshape(DEPTH, 1, N_MOD * d))


def _zoh_kernel(lr_ref, li_ref, ldt_ref, br_ref, bi_ref, abr_ref, abi_ref, bbr_ref, bbi_ref):
    dt = jnp.exp(ldt_ref[...])
    lr = lr_ref[...]
    li = li_ref[...]
    mag = jnp.exp(lr * dt)
    ab_re = mag * jnp.cos(li * dt)
    ab_im = mag * jnp.sin(li * dt)
    den = lr * lr + li * li
    f_re = ((ab_re - 1.0) * lr + ab_im * li) / den
    f_im = (ab_im * lr - (ab_re - 1.0) * li) / den
    br = br_ref[...]
    bi = bi_ref[...]
    abr_ref[...] = ab_re
    abi_ref[...] = ab_im
    bbr_ref[...] = f_re * br - f_im * bi
    bbi_ref[...] = f_re * bi + f_im * br


def _s5_tables(lam_re, lam_im, log_dt, b_re, b_im, c_re, c_im):
    g, h, p = S5_GROUPS, S5_CH, S5_STATE
    rows = DEPTH * 2 * g * h

    def rep(a):
        return jnp.broadcast_to(a[:, :, :, None, :], (DEPTH, 2, g, h, p)).reshape(rows, p)

    ldt = jnp.broadcast_to(log_dt[:, :, :, None, None], (DEPTH, 2, g, h, p)).reshape(rows, p)
    bt_re = jnp.swapaxes(b_re, -1, -2).reshape(rows, p)
    bt_im = jnp.swapaxes(b_im, -1, -2).reshape(rows, p)
    shp = jax.ShapeDtypeStruct((rows, p), F32)
    ab_re, ab_im, bb_re, bb_im = pl.pallas_call(
        _zoh_kernel, out_shape=(shp, shp, shp, shp), name="s5_zoh",
    )(rep(lam_re), rep(lam_im), ldt, bt_re, bt_im)

    def chan(a):
        return a.reshape(DEPTH, 2, g, h, p)[:, :, :, 0, :].reshape(DEPTH, 2, g * p)

    half = SUBLANES // 2
    avec = jnp.stack([jnp.repeat(chan(ab_re), half, axis=1), jnp.repeat(chan(ab_im), half, axis=1)], axis=1)
    eye = jnp.eye(g, dtype=F32)

    def in_blk(bt):
        bt = bt.reshape(DEPTH, 2, g, h, p)
        return jnp.einsum('gk,ldghp->ldghkp', eye, bt).reshape(DEPTH, 2, g * h, g * p)

    def out_blk(cc):
        return jnp.einsum('gk,ldkhp->ldgpkh', eye, cc).reshape(DEPTH, 2, g * p, g * h)

    bblk = jnp.concatenate([in_blk(bb_re), in_blk(bb_im)], axis=-1).astype(BF16)
    cblk = jnp.concatenate([out_blk(c_re), out_blk(-c_im)], axis=-2).astype(BF16)
    return avec, bblk, cblk


def _mod_spec(l, row_fn, sub):
    return pl.BlockSpec((None, None, None, 3, 1, D_MODEL), lambda i: (l, row_fn(i), sub, 0, 0, 0))


def _ffn_specs(l, sub, which):
    single = pl.Buffered(1)
    d = D_MODEL
    return [
        pl.BlockSpec((None, None, 1, d), lambda i: (l, sub, 0, 0)),
        pl.BlockSpec((None, None, d, D_FF), lambda i: (l, which, 0, 0), pipeline_mode=single),
        pl.BlockSpec((None, None, d, D_FF), lambda i: (l, which, 0, 0), pipeline_mode=single),
        pl.BlockSpec((None, None, D_FF, d), lambda i: (l, which, 0, 0), pipeline_mode=single),
    ]


def _ffn_compute(x, mod_ref, g_ref, wg_ref, wu_ref, wd_ref):
    half = x.shape[0] // 2
    xs = (x[:half], x[half:])
    hs = [_modulate(xh, g_ref[...], mod_ref[0], mod_ref[1]).astype(BF16) for xh in xs]
    ys = [None, None]
    for c0, c1 in zip(FF_SPLITS[:-1], FF_SPLITS[1:]):
        for r in range(2):
            a = _dot(hs[r], wg_ref[:, c0:c1])
            u = _dot(hs[r], wu_ref[:, c0:c1])
            part = _dot((_silu(a) * u).astype(BF16), wd_ref[c0:c1, :])
            ys[r] = part if ys[r] is None else ys[r] + part
    return jnp.concatenate([xs[r] + (0.5 * mod_ref[2]) * ys[r] for r in range(2)], axis=0)


def _ffn_inproj_kernel(x_ref, mod0_ref, g0_ref, wg_ref, wu_ref, wd_ref, mod1_ref, g1_ref, w_ref,
                       qn_ref, kn_ref, *rest, rope):
    if rope:
        cos_ref, sin_ref, y_ref, q_ref, k_ref, v_ref, p_ref, u_ref, us_ref = rest
    else:
        y_ref, q_ref, k_ref, v_ref, p_ref, u_ref, us_ref = rest
    x = _ffn_compute(x_ref[...], mod0_ref, g0_ref, wg_ref, wu_ref, wd_ref)
    y_ref[...] = x
    h = _modulate(x, g1_ref[...], mod1_ref[0], mod1_ref[1]).astype(BF16)
    proj = _dot(h, w_ref[...])

    lane = lax.broadcasted_iota(jnp.int32, (1, LANES), 1)
    low = lane < HEAD_DIM
    first_half = (lane & (AXIS_DIM - 1)) < AXIS_FREQS
    def head_norm(blk, w):
        width = blk.shape[1]
        er = lax.broadcasted_iota(jnp.int32, (width, width), 0) // HEAD_DIM
        ec = lax.broadcasted_iota(jnp.int32, (width, width), 1) // HEAD_DIM
        same_head = (er == ec).astype(BF16)
        sq = blk * blk
        hi = sq.astype(BF16)
        lo = (sq - hi.astype(F32)).astype(BF16)
        ss = _dot(hi, same_head) + _dot(lo, same_head)
        return blk * lax.rsqrt(ss * (1.0 / HEAD_DIM) + EPS) * w

    def rotary(y):
        if not rope:
            return y
        partner = jnp.where(first_half,
                            pltpu.roll(y, LANES - AXIS_FREQS, 1),
                            pltpu.roll(y, AXIS_FREQS, 1))
        return y * cos_ref[...] + partner * sin_ref[...]

    zero = jnp.zeros((), F32)
    q_scale = HEAD_DIM ** -0.5 * math.log2(math.e)
    qn2 = jnp.concatenate([qn_ref[...]] * (MXU_DIM // LANES), axis=-1)
    q_normed = [head_norm(proj[:, s * MXU_DIM:(s + 1) * MXU_DIM], qn2) for s in range(ATTN_W // MXU_DIM)]
    for j in range(N_Q_HEADS // 2):
        off = (j * LANES) % MXU_DIM
        y = rotary(q_normed[j * LANES // MXU_DIM][:, off:off + LANES]) * q_scale
        swapped = pltpu.roll(y, HEAD_DIM, 1)
        if j < N_Q_HEADS // 4:
            even, odd = jnp.where(low, y, zero), jnp.where(low, swapped, zero)
        else:
            even, odd = jnp.where(low, zero, swapped), jnp.where(low, zero, y)
        q_ref[(2 * j) * LANES:(2 * j + 1) * LANES, :] = even.T.astype(BF16)
        q_ref[(2 * j + 1) * LANES:(2 * j + 2) * LANES, :] = odd.T.astype(BF16)

    kblk = proj[:, ATTN_W:ATTN_W + KV_W]
    kk = rotary(head_norm(kblk, kn_ref[...])).astype(BF16)
    vblk = proj[:, ATTN_W + KV_W:ATTN_W + 2 * KV_W]
    ones = jnp.ones((V_ROWS - HEAD_DIM, KV_CHUNK), BF16)
    for c in range(k_ref.shape[0]):
        k_ref[c] = kk[c * KV_CHUNK:(c + 1) * KV_CHUNK, :]
        vt = vblk[c * KV_CHUNK:(c + 1) * KV_CHUNK, :].T.astype(BF16)
        for g in range(N_KV_HEADS):
            v_ref[c, g, 0:HEAD_DIM, :] = vt[g * HEAD_DIM:(g + 1) * HEAD_DIM, :]
            v_ref[c, g, HEAD_DIM:V_ROWS, :] = ones

    p_ref[...] = proj[:, ATTN_W + 2 * KV_W:ATTN_W + 2 * KV_W + POOL_W]
    u = proj[:, ATTN_W + 2 * KV_W + POOL_W:]
    u_ref[...] = u
    us_ref[...] = u.astype(BF16)


def _ffn_inproj(x, mods, norm_sub, wg, wu, wd, w_in, qn, kn, l, row_fn, tm, batch, rope_tabs):
    n, d = x.shape
    seg = n // batch
    tpb = seg // tm
    rope = rope_tabs is not None
    in_specs = (
        [pl.BlockSpec((tm, d), lambda i: (i, 0)), _mod_spec(l, row_fn, 0)]
        + _ffn_specs(l, 0, 0)
        + [_mod_spec(l, row_fn, 1),
           pl.BlockSpec((None, None, 1, d), lambda i: (l, 1, 0, 0)),
           pl.BlockSpec((None, d, IN_W), lambda i: (l, 0, 0), pipeline_mode=pl.Buffered(1)),
           pl.BlockSpec((None, 1, LANES), lambda i: (l, 0, 0)),
           pl.BlockSpec((None, 1, LANES), lambda i: (l, 0, 0))])
    args = [x, mods, norm_sub, wg, wu, wd, mods, norm_sub, w_in, qn, kn]
    if rope:
        in_specs += [pl.BlockSpec((tm, LANES), lambda i: (i % tpb, 0))] * 2
        args += list(rope_tabs)
    cpt = tm // KV_CHUNK
    out_shape = (
        jax.ShapeDtypeStruct((n, d), F32),
        jax.ShapeDtypeStruct((N_Q_HEADS * LANES, n), BF16),
        jax.ShapeDtypeStruct((n // KV_CHUNK, KV_CHUNK, KV_W), BF16),
        jax.ShapeDtypeStruct((n // KV_CHUNK, N_KV_HEADS, V_ROWS, KV_CHUNK), BF16),
        jax.ShapeDtypeStruct((n, POOL_W), F32),
        jax.ShapeDtypeStruct((seg, batch * S5_W), F32),
        jax.ShapeDtypeStruct((seg, batch * S5_W), BF16),
    )
    out_specs = (
        pl.BlockSpec((tm, d), lambda i: (i, 0)),
        pl.BlockSpec((N_Q_HEADS * LANES, tm), lambda i: (0, i)),
        pl.BlockSpec((cpt, KV_CHUNK, KV_W), lambda i: (i, 0, 0)),
        pl.BlockSpec((cpt, N_KV_HEADS, V_ROWS, KV_CHUNK), lambda i: (i, 0, 0, 0)),
        pl.BlockSpec((tm, POOL_W), lambda i: (i, 0)),
        pl.BlockSpec((tm, S5_W), lambda i: (i % tpb, i // tpb)),
        pl.BlockSpec((tm, S5_W), lambda i: (i % tpb, i // tpb)),
    )
    return pl.pallas_call(
        functools.partial(_ffn_inproj_kernel, rope=rope),
        grid=(n // tm,),
        in_specs=in_specs,
        out_specs=out_specs,
        out_shape=out_shape,
        compiler_params=_params(("arbitrary",)),
        name="ffn_inproj",
    )(*args)


def _attn_kernel(*refs, n_lat):
    if n_lat:
        q_ref, kc_ref, vc_ref, kl_ref, vl_ref, o_ref, m_ref, acc_ref, ot_ref, sa_ref, sb_ref = refs
    else:
        q_ref, kc_ref, vc_ref, o_ref, m_ref, acc_ref, ot_ref, sa_ref, sb_ref = refs
    m_ref[...] = jnp.full(m_ref.shape, -1e30, F32)
    acc_ref[...] = jnp.zeros(acc_ref.shape, F32)

    def step(v_of_head, cur_ref, k_next, nxt_ref):
        for h in range(N_Q_HEADS):
            if k_next is not None and h % 2 == 0:
                for hh in (h, h + 1):
                    nxt_ref[hh] = _dot(k_next, q_ref[hh * LANES:(hh + 1) * LANES, :])
            m_prev = m_ref[h]
            m_new = jnp.maximum(m_prev, jnp.max(cur_ref[h], axis=0, keepdims=True))
            alpha = jnp.exp2(m_prev - m_new)
            m_ref[h] = m_new
            pt = jnp.exp2(cur_ref[h] - m_new).astype(BF16)
            acc_ref[h] = alpha * acc_ref[h] + _dot(v_of_head(h // Q_PER_KV), pt)

    kc = kc_ref[0]
    for h in range(N_Q_HEADS):
        sa_ref[h] = _dot(kc, q_ref[h * LANES:(h + 1) * LANES, :])
    ctx_v = lambda g: vc_ref[0, g]
    if not n_lat:
        step(ctx_v, sa_ref, None, None)
    else:
        lat_v = lambda j: (lambda g: vl_ref[j, g])
        step(ctx_v, sa_ref, kl_ref[0], sb_ref)

        def body(i, carry):
            step(lat_v(2 * i), sb_ref, kl_ref[2 * i + 1], sa_ref)
            step(lat_v(2 * i + 1), sa_ref, kl_ref[2 * i + 2], sb_ref)
            return carry

        pairs = (n_lat - 1) // 2
        lax.fori_loop(0, pairs, body, 0)
        if n_lat - 2 * pairs == 2:
            step(lat_v(n_lat - 2), sb_ref, kl_ref[n_lat - 1], sa_ref)
            step(lat_v(n_lat - 1), sa_ref, None, None)
        else:
            step(lat_v(n_lat - 1), sb_ref, None, None)
    for h in range(N_Q_HEADS):
        ot_ref[h * HEAD_DIM:(h + 1) * HEAD_DIM, :] = (
            acc_ref[h, 0:HEAD_DIM, :] / acc_ref[h, HEAD_DIM:HEAD_DIM + 1, :])
    o_ref[...] = ot_ref[...].T


def _attention(qt, k_ctx, vt_ctx, k_lat, vt_lat, batch, tq):
    n = qt.shape[1]
    nq = n // batch // tq
    n_lat = 0 if k_lat is None else k_lat.shape[0] // batch
    in_specs = [
        pl.BlockSpec((N_Q_HEADS * LANES, tq), lambda b, i: (0, b * nq + i)),
        pl.BlockSpec((1, KV_CHUNK, KV_W), lambda b, i: (b, 0, 0)),
        pl.BlockSpec((1, N_KV_HEADS, V_ROWS, KV_CHUNK), lambda b, i: (b, 0, 0, 0)),
    ]
    args = [qt, k_ctx, vt_ctx]
    if n_lat:
        in_specs += [
            pl.BlockSpec((n_lat, KV_CHUNK, KV_W), lambda b, i: (b, 0, 0)),
            pl.BlockSpec((n_lat, N_KV_HEADS, V_ROWS, KV_CHUNK), lambda b, i: (b, 0, 0, 0)),
        ]
        args += [k_lat, vt_lat]
    return pl.pallas_call(
        functools.partial(_attn_kernel, n_lat=n_lat),
        grid=(batch, nq),
        in_specs=in_specs,
        out_specs=pl.BlockSpec((tq, ATTN_W), lambda b, i: (b * nq + i, 0)),
        out_shape=jax.ShapeDtypeStruct((n, ATTN_W), F32),
        scratch_shapes=[
            pltpu.VMEM((N_Q_HEADS, 1, tq), F32),
            pltpu.VMEM((N_Q_HEADS, V_ROWS, tq), F32),
            pltpu.VMEM((ATTN_W, tq), F32),
            pltpu.VMEM((N_Q_HEADS, KV_CHUNK, tq), F32),
            pltpu.VMEM((N_Q_HEADS, KV_CHUNK, tq), F32),
        ],
        compiler_params=_params(("arbitrary", "arbitrary")),
        name="attention",
    )(*args)


def _s5_kernel(ufl_ref, ufc_ref, ubl_ref, ubc_ref, a_ref, bblk_ref, cblk_ref, yf_ref, yb_ref,
               sf_ref, sb_ref, st_ref, *, tc, batch, n_ctx_tiles, n_tiles):
    nch = S5_CHANNELS
    i = pl.program_id(0)

    @pl.when(i == 0)
    def _():
        st_ref[...] = jnp.zeros(st_ref.shape, F32)

    assert 2 * batch == SUBLANES
    ngrp = tc * batch // SUBLANES
    uf = jnp.where(i < n_ctx_tiles, ufc_ref[...], ufl_ref[...])
    ub = jnp.where(i < n_ctx_tiles, ubc_ref[...], ubl_ref[...])
    sf_ref[...] = _dot(uf.astype(BF16), bblk_ref[0]).reshape(ngrp, SUBLANES, 2 * nch)
    sb_ref[...] = _dot(ub.astype(BF16), bblk_ref[1]).reshape(ngrp, SUBLANES, 2 * nch)
    a_re, a_im = a_ref[0], a_ref[1]
    top = lax.broadcasted_iota(jnp.int32, (SUBLANES, 1), 0) < batch

    def advance(s_re, s_im, x):
        return (a_re * s_re - a_im * s_im + x[:, :nch], a_re * s_im + a_im * s_re + x[:, nch:])

    def body(g, carry):
        s_re, s_im = carry
        gb = ngrp - 1 - g
        f = sf_ref[g]
        b = sb_ref[gb]
        first = jnp.where(top, f, b)
        second = pltpu.roll(jnp.where(top, b, f), batch, 0)
        r1, i1 = advance(s_re, s_im, first)
        r2, i2 = advance(r1, i1, second)
        s1 = jnp.concatenate([r1, i1], axis=1)
        s2 = pltpu.roll(jnp.concatenate([r2, i2], axis=1), batch, 0)
        sf_ref[g] = jnp.where(top, s1, s2)
        sb_ref[gb] = jnp.where(top, s2, s1)
        return r2, i2

    s_re, s_im = lax.fori_loop(0, ngrp, body, (st_ref[0], st_ref[1]))
    st_ref[0] = s_re
    st_ref[1] = s_im
    rows = ngrp * SUBLANES
    yf_ref[...] = _dot(sf_ref[...].reshape(rows, 2 * nch).astype(BF16), cblk_ref[0])
    yb_ref[...] = _dot(sb_ref[...].reshape(rows, 2 * nch).astype(BF16), cblk_ref[1])


def _s5_scan(u_lat, u_ctx, avec, bblk, cblk, l, batch, tc):
    rows = tc * batch
    n_lat_tiles = u_lat.shape[0] // rows
    n_ctx_tiles = u_ctx.shape[0] // rows
    n_tiles = n_lat_tiles + n_ctx_tiles
    nch = S5_CHANNELS

    def fwd(i):
        return jnp.where(i < n_ctx_tiles, n_lat_tiles + i, i - n_ctx_tiles)

    def bwd(i):
        return n_tiles - 1 - i

    def lat_of(tile):
        return lambda i: (jnp.clip(tile(i), 0, n_lat_tiles - 1), 0)

    def ctx_of(tile):
        return lambda i: (jnp.clip(tile(i) - n_lat_tiles, 0, n_ctx_tiles - 1), 0)

    blk = (rows, S5_W)
    shp = jax.ShapeDtypeStruct((n_tiles * rows, S5_W), F32)
    return pl.pallas_call(
        functools.partial(_s5_kernel, tc=tc, batch=batch, n_ctx_tiles=n_ctx_tiles, n_tiles=n_tiles),
        grid=(n_tiles,),
        in_specs=[
            pl.BlockSpec(blk, lat_of(fwd)), pl.BlockSpec(blk, ctx_of(fwd)),
            pl.BlockSpec(blk, lat_of(bwd)), pl.BlockSpec(blk, ctx_of(bwd)),
            pl.BlockSpec((None, 2, SUBLANES, nch), lambda i: (l, 0, 0, 0)),
            pl.BlockSpec((None, 2, S5_W, 2 * nch), lambda i: (l, 0, 0, 0)),
            pl.BlockSpec((None, 2, 2 * nch, S5_W), lambda i: (l, 0, 0, 0)),
        ],
        out_specs=(
            pl.BlockSpec(blk, lambda i: (fwd(i), 0)),
            pl.BlockSpec(blk, lambda i: (bwd(i), 0)),
        ),
        out_shape=(shp, shp),
        scratch_shapes=[
            pltpu.VMEM((rows // SUBLANES, SUBLANES, 2 * nch), F32),
            pltpu.VMEM((rows // SUBLANES, SUBLANES, 2 * nch), F32),
            pltpu.VMEM((2, SUBLANES, nch), F32),
        ],
        compiler_params=_params(("arbitrary",)),
        name="s5_scan",
    )(u_lat, u_ctx, u_lat, u_ctx, avec, bblk, cblk)


def _gelu_tanh(x):
    return 0.5 * x * (1.0 + jnp.tanh(math.sqrt(2.0 / math.pi) * (x + 0.044715 * (x * x * x))))


def _merge_ffn_kernel(x_ref, mod1_ref, attn_ref, pin_ref, prev_ref, next_ref, u_ref, yf_ref, yb_ref,
                      an_ref, sn_ref, pw_ref, ps_ref, dsk_ref, gw_ref, gb_ref, wo_ref,
                      mod2_ref, g2_ref, wg_ref, wu_ref, wd_ref, *rest, tm, tpb, seg, final):
    if final:
        fg_ref, o_ref, ext_ref = rest
    else:
        o_ref, ext_ref = rest
    ti = pl.program_id(0) % tpb
    a_n = _rmsnorm(attn_ref[...], an_ref[...])

    hal = POOL_HALO
    ext_ref[pl.ds(0, hal), :] = jnp.where(ti > 0, prev_ref[...], 0.0)
    ext_ref[pl.ds(hal, tm), :] = pin_ref[...]
    ext_ref[pl.ds(hal + tm, hal), :] = jnp.where(ti < tpb - 1, next_ref[...], 0.0)
    t = ti * tm + lax.broadcasted_iota(jnp.int32, (tm, 1), 0)
    low = lax.broadcasted_iota(jnp.int32, (1, LANES), 1) < POOL_CH

    def shifted(o, c):
        return ext_ref[pl.ds(hal + o, tm), pl.ds(c * LANES, LANES)]

    def count(w):
        hi = jnp.minimum(t + (w - w // 2), seg)
        lo = jnp.maximum(t - w // 2, 0)
        return (hi - lo).astype(F32)

    def window_sums(c, w_small, w_big):
        acc = shifted(0, c)
        for o in range(-(w_small // 2), w_small - w_small // 2):
            if o != 0:
                acc = acc + shifted(o, c)
        small = acc
        for o in range(-(w_big // 2), w_big - w_big // 2):
            if not (-(w_small // 2) <= o < w_small - w_small // 2):
                acc = acc + shifted(o, c)
        return jnp.where(low, small / count(w_small), acc / count(w_big))

    pooled = jnp.concatenate(
        [window_sums(0, POOL_WINDOWS[0], POOL_WINDOWS[1]),
         window_sums(1, POOL_WINDOWS[2], POOL_WINDOWS[3])], axis=-1) - pin_ref[...]
    p_out = _dot(pooled.astype(BF16), pw_ref[...]) * ps_ref[...]

    y = dsk_ref[...] * u_ref[...] + yf_ref[...] + yb_ref[...]
    g = _gelu_tanh(y)
    glu = g * jax.nn.sigmoid(_dot(g.astype(BF16), gw_ref[...]) + gb_ref[...])
    s_n = _rmsnorm(glu, sn_ref[...])

    cat = jnp.concatenate([a_n, p_out, s_n], axis=-1).astype(BF16)
    x = x_ref[...] + mod1_ref[2] * _dot(cat, wo_ref[...])
    x = _ffn_compute(x, mod2_ref, g2_ref, wg_ref, wu_ref, wd_ref)
    if final:
        x = _rmsnorm(x, fg_ref[...])
    o_ref[...] = x


def _merge_ffn(x, mods, norm_sub, attn, pin, u_tm, yf_tm, yb_tm, y_row_off, an, sn, pw, ps, dsk, gw, gb, wo,
               wg, wu, wd, l, row_fn, tm, batch, final_g=None):
    n, d = x.shape
    seg = n // batch
    tpb = seg // tm
    hal = POOL_HALO
    nhb = n // hal
    yo = y_row_off // tm
    single = pl.Buffered(1)

    def tmaj(off):
        return pl.BlockSpec((tm, S5_W), lambda i: (off + i % tpb, i // tpb))

    def vec(width):
        return pl.BlockSpec((None, 1, width), lambda i: (l, 0, 0))

    def mat(r, c):
        return pl.BlockSpec((None, r, c), lambda i: (l, 0, 0), pipeline_mode=single)

    in_specs = [
        pl.BlockSpec((tm, d), lambda i: (i, 0)),
        _mod_spec(l, row_fn, 1),
        pl.BlockSpec((tm, ATTN_W), lambda i: (i, 0)),
        pl.BlockSpec((tm, POOL_W), lambda i: (i, 0)),
        pl.BlockSpec((hal, POOL_W), lambda i: (jnp.maximum(i * (tm // hal) - 1, 0), 0)),
        pl.BlockSpec((hal, POOL_W), lambda i: (jnp.minimum((i + 1) * (tm // hal), nhb - 1), 0)),
        tmaj(0), tmaj(yo), tmaj(yo),
        vec(ATTN_W), vec(S5_W), mat(POOL_W, POOL_W), vec(POOL_W), vec(S5_W), mat(S5_W, S5_W), vec(S5_W),
        mat(d, d),
        _mod_spec(l, row_fn, 2),
    ] + _ffn_specs(l, 2, 1)
    args = [x, mods, attn, pin, pin, pin, u_tm, yf_tm, yb_tm, an, sn, pw, ps, dsk, gw, gb, wo,
            mods, norm_sub, wg, wu, wd]
    if final_g is not None:
        in_specs.append(pl.BlockSpec((1, d), lambda i: (0, 0)))
        args.append(final_g.reshape(1, d))
    return pl.pallas_call(
        functools.partial(_merge_ffn_kernel, tm=tm, tpb=tpb, seg=seg, final=final_g is not None),
        grid=(n // tm,),
        in_specs=in_specs,
        out_specs=pl.BlockSpec((tm, d), lambda i: (i, 0)),
        out_shape=jax.ShapeDtypeStruct((n, d), F32),
        scratch_shapes=[pltpu.VMEM((tm + 2 * hal, POOL_W), F32)],
        compiler_params=_params(("arbitrary",)),
        name="merge_ffn",
    )(*args)


def _rope_tables(seq):
    rows = seq // GRID_W
    row = jnp.broadcast_to(jnp.arange(rows, dtype=F32)[:, None], (rows, GRID_W)).reshape(-1)
    col = jnp.broadcast_to(jnp.arange(GRID_W, dtype=F32)[None, :], (rows, GRID_W)).reshape(-1)
    inv = ROPE_THETA ** (-2.0 * jnp.arange(AXIS_FREQS, dtype=F32) / AXIS_DIM)
    ang = jnp.concatenate([row[:, None] * inv, col[:, None] * inv], axis=-1)
    cos, sin = jnp.cos(ang), jnp.sin(ang)
    dim = jnp.arange(LANES) % HEAD_DIM
    idx = (dim // AXIS_DIM) * AXIS_FREQS + dim % AXIS_FREQS
    sign = jnp.where((dim % AXIS_DIM) < AXIS_FREQS, -1.0, 1.0).astype(F32)
    return cos[:, idx], sin[:, idx] * sign


def _block_diag(w):
    dep, g, c, c2 = w.shape
    eye = jnp.eye(g, dtype=w.dtype)
    return jnp.einsum('gk,lgcd->lgckd', eye, w).reshape(dep, g * c, g * c2)


def kernel(x, c, ctx, c_ctx, w_ada, b_ada, norm_sub, w_ffn_gate, w_ffn_up, w_ffn_down, w_in, w_out, q_norm, k_norm, attn_out_norm, ssm_out_norm, pool_w, pool_scale, s5_lam_re, s5_lam_im, s5_log_dt, s5_b_re, s5_b_im, s5_c_re, s5_c_im, s5_d, s5_glu_w, s5_glu_b, final_norm):
    batch, seq, d = x.shape
    ctx_len = ctx.shape[1]
    tm_lat, tm_ctx = 512, 256
    tq = 512
    tc = 256
    lat_tpb = seq // tm_lat

    cond = jnp.concatenate([c_ctx[None], c, jnp.zeros((SUBLANES - 1 - batch, d), F32)], axis=0)
    mods = _adaln(cond, w_ada, b_ada).reshape(DEPTH, SUBLANES, 3, 3, 1, d)
    lat_row = lambda i: 1 + i // lat_tpb
    ctx_row = lambda i: 0

    nsub = norm_sub.reshape(DEPTH, 3, 1, d)
    wg, wu, wd = w_ffn_gate.astype(BF16), w_ffn_up.astype(BF16), w_ffn_down.astype(BF16)
    w_in_b, w_out_b = w_in.astype(BF16), w_out.astype(BF16)
    qn = jnp.tile(q_norm, (1, LANES // HEAD_DIM)).reshape(DEPTH, 1, LANES)
    kn = jnp.tile(k_norm, (1, LANES // HEAD_DIM)).reshape(DEPTH, 1, LANES)
    an = attn_out_norm.reshape(DEPTH, 1, ATTN_W)
    sn = ssm_out_norm.reshape(DEPTH, 1, S5_W)
    pw = _block_diag(pool_w).astype(BF16)
    ps = pool_scale.reshape(DEPTH, 1, POOL_W)
    dsk = s5_d.reshape(DEPTH, 1, S5_W)
    gw = _block_diag(s5_glu_w).astype(BF16)
    gb = s5_glu_b.reshape(DEPTH, 1, S5_W)
    avec, bblk, cblk = _s5_tables(s5_lam_re, s5_lam_im, s5_log_dt, s5_b_re, s5_b_im, s5_c_re, s5_c_im)
    rope_tabs = _rope_tables(seq)

    xs = x.reshape(batch * seq, d)
    zs = ctx.reshape(batch * ctx_len, d)
    for l in range(DEPTH):
        last = l == DEPTH - 1
        xs, qx, kx, vx, px, ux, usx = _ffn_inproj(xs, mods, nsub, wg, wu, wd, w_in_b, qn, kn, l, lat_row,
                                             tm_lat, batch, rope_tabs)
        zs, qz, kz, vz, pz, uz, usz = _ffn_inproj(zs, mods, nsub, wg, wu, wd, w_in_b, qn, kn, l, ctx_row,
                                             tm_ctx, batch, None)

        attn_x = _attention(qx, kz, vz, kx, vx, batch, tq)

        yf, yb = _s5_scan(usx.reshape(seq * batch, S5_W), usz.reshape(ctx_len * batch, S5_W),
                          avec, bblk, cblk, l, batch, tc)
        yf = yf.reshape(seq + ctx_len, batch * S5_W)
        yb = yb.reshape(seq + ctx_len, batch * S5_W)

        xs = _merge_ffn(xs, mods, nsub, attn_x, px, ux, yf, yb, 0, an, sn, pw, ps, dsk, gw, gb, w_out_b,
                        wg, wu, wd, l, lat_row, tm_lat, batch, final_g=final_norm if last else None)
        if not last:
            attn_z = _attention(qz, kz, vz, None, None, batch, tm_ctx)
            zs = _merge_ffn(zs, mods, nsub, attn_z, pz, uz, yf, yb, seq, an, sn, pw, ps, dsk, gw, gb,
                            w_out_b, wg, wu, wd, l, ctx_row, tm_ctx, batch)
    return xs.reshape(batch, seq, d)
```
